```python
import math
import jax, jax.numpy as jnp
from jax import lax
import numpy as np

D_MODEL = 2048
BATCH = 1
SEQ = 8192
DEPTH = 2
DEC_BATCH = 8
DEC_SEQ = 16
PAST_LEN = 4096

CHUNK = 64
QBLOCK = 128
HEAD_DIM = 128
EPS = 1e-6
POOL_WIDTH = D_MODEL // 2
POOL_WINDOWS = (2, 4, 8, 16)
N_POOL_GROUPS = len(POOL_WINDOWS)
POOL_GROUP = POOL_WIDTH // N_POOL_GROUPS
POOL_STATE = max(POOL_WINDOWS) - 1
FOX_WIDTH = D_MODEL - POOL_WIDTH
FOX_HEADS = FOX_WIDTH // HEAD_DIM
SB_HEADS = D_MODEL // HEAD_DIM
IN_AB = POOL_WIDTH + 3 * FOX_WIDTH + FOX_HEADS
D_FF = -(-8 * D_MODEL // (3 * 256)) * 256
N_EVEN = (DEPTH + 1) // 2
N_ODD = DEPTH // 2

kernel_name = 'pool_fox_stickbreak_streaming_encoder_step'


def rmsnorm(x, g):
    xf = x.astype(jnp.float32)
    y = xf * lax.rsqrt(jnp.mean(xf * xf, axis=-1, keepdims=True) + EPS)
    return (y * g.astype(jnp.float32)).astype(x.dtype)


def swiglu(h, w_gate, w_up, w_down):
    return (jax.nn.silu(h @ w_gate) * (h @ w_up)) @ w_down


def pool_mixer(u, prefix, pos, w_pool, pool_scale):
    b, t, _ = u.shape
    up = jnp.concatenate([prefix.astype(u.dtype), u], axis=1)
    cs = jnp.pad(jnp.cumsum(up.astype(jnp.float32), axis=1), ((0, 0), (1, 0), (0, 0)))
    end = cs[:, POOL_STATE + 1:]
    means = []
    for g, w in enumerate(POOL_WINDOWS):
        ch = slice(g * POOL_GROUP, (g + 1) * POOL_GROUP)
        start = cs[:, POOL_STATE + 1 - w:POOL_STATE + 1 - w + t, ch]
        count = jnp.minimum(pos + 1, w).astype(jnp.float32)[None, :, None]
        means.append((end[:, :, ch] - start) / count)
    d = jnp.concatenate(means, axis=-1) - u.astype(jnp.float32)
    d = d.astype(u.dtype).reshape(b, t, N_POOL_GROUPS, POOL_GROUP)
    y = jnp.einsum('btgc,gce->btge', d, w_pool).reshape(b, t, POOL_WIDTH)
    return y * pool_scale, up[:, -POOL_STATE:]


def fox_block(q, cum_q, q_pos, k, v, cum_k, k_pos):
    s = jnp.einsum('bqhd,bkhd->bhqk', q, k, preferred_element_type=jnp.float32) * (HEAD_DIM ** -0.5)
    s = s + jnp.transpose(cum_q, (0, 2, 1))[..., None] - jnp.transpose(cum_k, (0, 2, 1))[:, :, None, :]
    mask = k_pos[None, :] <= q_pos[:, None]
    p = jax.nn.softmax(jnp.where(mask, s, -jnp.inf), axis=-1)
    return jnp.einsum('bhqk,bkhd->bqhd', p.astype(v.dtype), v)


def sb_block(q, q_pos, k, v, k_pos):
    z = jnp.einsum('bqhd,bkhd->bhqk', q, k, preferred_element_type=jnp.float32) * (HEAD_DIM ** -0.5)
    mask = k_pos[None, :] < q_pos[:, None]
    log_beta = jax.nn.log_sigmoid(z)
    log_rest = jnp.where(mask, jax.nn.log_sigmoid(-z), 0.0)
    later = lax.cumsum(log_rest, axis=3, reverse=True) - log_rest
    a = jnp.where(mask, jnp.exp(log_beta + later), 0.0)
    return jnp.einsum('bhqk,bkhd->bqhd', a.astype(v.dtype), v)


def sweep_queries(fn, q_args, q_pos, kv_args):
    t = q_pos.shape[0]
    if t <= QBLOCK:
        return fn(*q_args, q_pos, *kv_args)
    nb = t // QBLOCK

    def split(a):
        return jnp.moveaxis(a.reshape(a.shape[0], nb, QBLOCK, *a.shape[2:]), 1, 0)

    blocks = tuple(split(a) for a in q_args) + (q_pos.reshape(nb, QBLOCK),)
    out = lax.map(lambda blk: fn(*blk[:-1], blk[-1], *kv_args), blocks)
    return jnp.moveaxis(out, 0, 1).reshape(out.shape[1], t, *out.shape[3:])


def pool_fox_mixer(h, pos, pool_prev, k_prev, v_prev, lf_prev, w_in, b_f, w_pool, pool_scale, w_out):
    b, t, _ = h.shape
    z = h @ w_in
    cuts = [POOL_WIDTH, POOL_WIDTH + FOX_WIDTH, POOL_WIDTH + 2 * FOX_WIDTH, POOL_WIDTH + 3 * FOX_WIDTH]
    u, q, k, v, f_logit = jnp.split(z, cuts, axis=-1)
    y_pool, pool_state = pool_mixer(u, pool_prev, pos, w_pool, pool_scale)
    q = q.reshape(b, t, FOX_HEADS, HEAD_DIM)
    k = k.reshape(b, t, FOX_HEADS, HEAD_DIM)
    v = v.reshape(b, t, FOX_HEADS, HEAD_DIM)
    log_f = jax.nn.log_sigmoid((f_logit + b_f).astype(jnp.float32))
    k_all = jnp.concatenate([k_prev.astype(k.dtype), k], axis=1)
    v_all = jnp.concatenate([v_prev.astype(v.dtype), v], axis=1)
    cum_f = jnp.cumsum(jnp.concatenate([lf_prev.astype(jnp.float32), log_f], axis=1), axis=1)
    n_past = k_prev.shape[1]
    k_pos = jnp.arange(n_past + t)
    o = sweep_queries(fox_block, (q, cum_f[:, n_past:]), pos, (k_all, v_all, cum_f, k_pos))
    y = jnp.concatenate([y_pool, o.reshape(b, t, FOX_WIDTH).astype(y_pool.dtype)], axis=-1) @ w_out
    return y, pool_state, k, v, log_f


def sb_mixer(h, pos, k_prev, v_prev, w_qkv, w_out):
    b, t, _ = h.shape
    q, k, v = jnp.split(h @ w_qkv, 3, axis=-1)
    q = q.reshape(b, t, SB_HEADS, HEAD_DIM)
    k = k.reshape(b, t, SB_HEADS, HEAD_DIM)
    v = v.reshape(b, t, SB_HEADS, HEAD_DIM)
    k_all = jnp.concatenate([k_prev.astype(k.dtype), k], axis=1)
    v_all = jnp.concatenate([v_prev.astype(v.dtype), v], axis=1)
    k_pos = jnp.arange(k_prev.shape[1] + t)
    o = sweep_queries(sb_block, (q,), pos, (k_all, v_all, k_pos))
    return o.reshape(b, t, D_MODEL) @ w_out, k, v


def run_trunk(x, pos, pool_c, fk_c, fv_c, flf_c, sk_c, sv_c, ln_mix, w_in_ab, b_forget, w_pool,
              pool_scale, w_out_ab, w_qkv_sb, w_out_sb, ln_ffn, w_gate, w_up, w_down, ln_final):
    h = x
    pool_s, fk_s, fv_s, flf_s, sk_s, sv_s = [], [], [], [], [], []
    for layer in range(DEPTH):
        hn = rmsnorm(h, ln_mix[layer])
        i = layer // 2
        if layer % 2 == 0:
            y, ps, k, v, lf = pool_fox_mixer(hn, pos, pool_c[i], fk_c[i], fv_c[i], flf_c[i], w_in_ab[i],
                                             b_forget[i], w_pool[i], pool_scale[i], w_out_ab[i])
            pool_s.append(ps)
            fk_s.append(k)
            fv_s.append(v)
            flf_s.append(lf)
        else:
            y, k, v = sb_mixer(hn, pos, sk_c[i], sv_c[i], w_qkv_sb[i], w_out_sb[i])
            sk_s.append(k)
            sv_s.append(v)
        h = h + y
        h = h + swiglu(rmsnorm(h, ln_ffn[layer]), w_gate[layer], w_up[layer], w_down[layer])
    return (rmsnorm(h, ln_final), jnp.stack(pool_s), jnp.stack(fk_s), jnp.stack(fv_s),
            jnp.stack(flf_s), jnp.stack(sk_s), jnp.stack(sv_s))


def setup_inputs(seed: int = 0) -> dict:
    key = jax.random.key(seed)
    keys = iter(jax.random.split(key, 32))

    def nrm(shape, scale=1.0):
        return scale * jax.random.normal(next(keys), shape, jnp.float32)

    return {
        'x_prompt': nrm((BATCH, SEQ, D_MODEL)),
        'x_sample': nrm((DEC_BATCH, DEC_SEQ, D_MODEL)),
        'cache_pool': nrm((N_EVEN, DEC_BATCH, POOL_STATE, POOL_WIDTH)),
        'cache_fox_k': nrm((N_EVEN, DEC_BATCH, PAST_LEN, FOX_HEADS, HEAD_DIM)),
        'cache_fox_v': nrm((N_EVEN, DEC_BATCH, PAST_LEN, FOX_HEADS, HEAD_DIM)),
        'cache_fox_logf': jax.nn.log_sigmoid(2.0 + nrm((N_EVEN, DEC_BATCH, PAST_LEN, FOX_HEADS), 0.5)),
        'cache_sb_k': nrm((N_ODD, DEC_BATCH, PAST_LEN, SB_HEADS, HEAD_DIM)),
        'cache_sb_v': nrm((N_ODD, DEC_BATCH, PAST_LEN, SB_HEADS, HEAD_DIM)),
        'ln_mix': 1.0 + nrm((DEPTH, D_MODEL), 0.05),
        'w_in_ab': nrm((N_EVEN, D_MODEL, IN_AB), D_MODEL ** -0.5),
        'b_forget': 2.0 + nrm((N_EVEN, FOX_HEADS), 0.5),
        'w_pool': nrm((N_EVEN, N_POOL_GROUPS, POOL_GROUP, POOL_GROUP), POOL_GROUP ** -0.5),
        'pool_scale': 1.0 + nrm((N_EVEN, POOL_WIDTH), 0.05),
        'w_out_ab': nrm((N_EVEN, D_MODEL, D_MODEL), D_MODEL ** -0.5),
        'w_qkv_sb': nrm((N_ODD, D_MODEL, 3 * D_MODEL), D_MODEL ** -0.5),
        'w_out_sb': nrm((N_ODD, D_MODEL, D_MODEL), D_MODEL ** -0.5),
        'ln_ffn': 1.0 + nrm((DEPTH, D_MODEL), 0.05),
        'w_gate': nrm((DEPTH, D_MODEL, D_FF), D_MODEL ** -0.5),
        'w_up': nrm((DEPTH, D_MODEL, D_FF), D_MODEL ** -0.5),
        'w_down': nrm((DEPTH, D_FF, D_MODEL), D_FF ** -0.5),
        'ln_final': 1.0 + nrm((D_MODEL,), 0.05),
    }


def reference(x_prompt, x_sample, cache_pool, cache_fox_k, cache_fox_v, cache_fox_logf, cache_sb_k,
              cache_sb_v, ln_mix, w_in_ab, b_forget, w_pool, pool_scale, w_out_ab, w_qkv_sb, w_out_sb,
              ln_ffn, w_gate, w_up, w_down, ln_final):
    if x_sample.shape[1] > CHUNK:
        raise ValueError('a later request holds at most one chunk of frames')
    b, s, _ = x_prompt.shape
    dt = x_prompt.dtype
    weights = (ln_mix, w_in_ab, b_forget, w_pool, pool_scale, w_out_ab, w_qkv_sb, w_out_sb,
               ln_ffn, w_gate, w_up, w_down, ln_final)
    y_p, pool_p, fk_p, fv_p, flf_p, sk_p, sv_p = run_trunk(
        x_prompt, jnp.arange(s),
        jnp.zeros((N_EVEN, b, POOL_STATE, POOL_WIDTH), dt),
        jnp.zeros((N_EVEN, b, 0, FOX_HEADS, HEAD_DIM), dt),
        jnp.zeros((N_EVEN, b, 0, FOX_HEADS, HEAD_DIM), dt),
        jnp.zeros((N_EVEN, b, 0, FOX_HEADS), jnp.float32),
        jnp.zeros((N_ODD, b, 0, SB_HEADS, HEAD_DIM), dt),
        jnp.zeros((N_ODD, b, 0, SB_HEADS, HEAD_DIM), dt),
        *weights)
    n_past = cache_fox_k.shape[2]
    y_s, pool_s, fk_s, fv_s, flf_s, sk_s, sv_s = run_trunk(
        x_sample, n_past + jnp.arange(x_sample.shape[1]),
        cache_pool, cache_fox_k, cache_fox_v, cache_fox_logf, cache_sb_k, cache_sb_v,
        *weights)
    return (y_p, y_s, pool_p, fk_p, fv_p, flf_p, sk_p, sv_p, pool_s, fk_s, fv_s, flf_s, sk_s, sv_s)
```

```python
import functools

import numpy as np
import jax
import jax.numpy as jnp
from jax import lax
from jax.experimental import pallas as pl
from jax.experimental.pallas import tpu as pltpu

HEAD_DIM = 128
EPS = 1e-6
POOL_WINDOWS = (2, 4, 8, 16)
POOL_STATE = max(POOL_WINDOWS) - 1
POOL_HALO = 16
LANES = 128
TRI = 256
VMEM_LIMIT = 56 * 1024 * 1024

F32 = jnp.float32
BF16 = jnp.bfloat16


def _cparams(sem, vmem=VMEM_LIMIT):
    return pltpu.CompilerParams(dimension_semantics=sem, vmem_limit_bytes=vmem)


def _resident(shape, index_map):
    return pl.BlockSpec(shape, index_map, pipeline_mode=pl.Buffered(1))


def _dot(a, b):
    return jnp.dot(a, b, preferred_element_type=F32)


def _dot_nt(a, b):
    return lax.dot_general(a, b, (((1,), (1,)), ((), ())), preferred_element_type=F32)


def _rms(x, g):
    ms = jnp.mean(x * x, axis=-1, keepdims=True)
    return x * lax.rsqrt(ms + EPS) * g


def _softplus(z):
    return jnp.maximum(z, 0.0) + jnp.log1p(jnp.exp(-jnp.abs(z)))


def _tile_lanes(x, n):
    reps = n // LANES
    return x if reps == 1 else jnp.concatenate([x] * reps, axis=1)


def _norm_proj_kernel(x_ref, g_ref, w_ref, b_ref, *out_refs, groups):
    xn = _rms(x_ref[...], g_ref[...]).astype(BF16)
    for w_col, width, outs in groups:
        acc = _dot(xn, w_ref[:, w_col:w_col + width])
        for out_idx, out_col, kind in outs:
            if kind == "logsig":
                val = -_softplus(-(acc + b_ref[...]))
            else:
                val = acc
            o = out_refs[out_idx]
            o[:, out_col:out_col + width] = val.astype(o.dtype)


def norm_proj(x, g, w, bias, groups, out_defs, tm):
    m, d = x.shape
    n = w.shape[1]
    tm = min(tm, m)
    out_shape = [jax.ShapeDtypeStruct((m, c), dt) for c, dt in out_defs]
    out_specs = [pl.BlockSpec((tm, c), lambda i: (i, 0)) for c, _ in out_defs]
    return pl.pallas_call(
        functools.partial(_norm_proj_kernel, groups=groups),
        grid=(m // tm,),
        in_specs=[
            pl.BlockSpec((tm, d), lambda i: (i, 0)),
            _resident((1, d), lambda i: (0, 0)),
            _resident((d, n), lambda i: (0, 0)),
            _resident((1, LANES), lambda i: (0, 0)),
        ],
        out_specs=out_specs,
        out_shape=out_shape,
        compiler_params=_cparams(("parallel",)),
        name="norm_proj",
    )(x, g, w, bias)


def _cumsum_kernel(x_ref, o_ref, carry_ref, *, tb):
    @pl.when(pl.program_id(0) == 0)
    def _():
        carry_ref[...] = jnp.zeros_like(carry_ref)

    x = x_ref[...]
    row = lax.broadcasted_iota(jnp.int32, (tb, tb), 0)
    col = lax.broadcasted_iota(jnp.int32, (tb, tb), 1)
    low = jnp.where(col <= row, 1.0, 0.0).astype(BF16)
    hi = x.astype(BF16)
    r1 = x - hi.astype(F32)
    mid = r1.astype(BF16)
    lo = (r1 - mid.astype(F32)).astype(BF16)
    out = _dot(low, hi) + _dot(low, mid) + _dot(low, lo) + carry_ref[0:1, :]
    o_ref[...] = out
    carry_ref[...] = jnp.broadcast_to(out[tb - 1:tb, :], carry_ref.shape)


def cumsum_rows(x, tb=512):
    t, c = x.shape
    tb = min(tb, t)
    return pl.pallas_call(
        functools.partial(_cumsum_kernel, tb=tb),
        grid=(t // tb,),
        in_specs=[pl.BlockSpec((tb, c), lambda i: (i, 0))],
        out_specs=pl.BlockSpec((tb, c), lambda i: (i, 0)),
        out_shape=jax.ShapeDtypeStruct((t, c), F32),
        scratch_shapes=[pltpu.VMEM((8, c), F32)],
        compiler_params=_cparams(("arbitrary",)),
        name="cumsum_rows",
    )(x)


def _pool_kernel(u_ref, pre_ref, w_ref, sc_ref, o_ref, buf_ref, *, tm, pos0, group):
    mi = pl.program_id(1)

    @pl.when(mi == 0)
    def _():
        buf_ref[0:POOL_HALO, :] = pre_ref[...]

    @pl.when(mi > 0)
    def _():
        buf_ref[0:POOL_HALO, :] = buf_ref[tm:tm + POOL_HALO, :]

    buf_ref[POOL_HALO:POOL_HALO + tm, :] = u_ref[...]
    pos = pos0 + mi * tm + lax.broadcasted_iota(jnp.int32, (tm, 1), 0)
    for gi, win in enumerate(POOL_WINDOWS):
        cs = slice(gi * group, (gi + 1) * group)
        tot = buf_ref[POOL_HALO:POOL_HALO + tm, cs]
        for back in range(1, win):
            tot = tot + buf_ref[POOL_HALO - back:POOL_HALO - back + tm, cs]
        count = jnp.minimum(pos + 1, win).astype(F32)
        diff = tot / count - u_ref[:, cs]
        y = _dot(diff.astype(BF16), w_ref[gi]) * sc_ref[:, cs]
        o_ref[:, cs] = y.astype(o_ref.dtype)


def pool_mixer(u, prefix, w_pool, scale, pos0, tm=512):
    b, t, c = u.shape
    tm = min(tm, t)
    group = c // len(POOL_WINDOWS)
    return pl.pallas_call(
        functools.partial(_pool_kernel, tm=tm, pos0=pos0, group=group),
        grid=(b, t // tm),
        in_specs=[
            pl.BlockSpec((None, tm, c), lambda bi, mi: (bi, mi, 0)),
            pl.BlockSpec((None, POOL_HALO, c), lambda bi, mi: (bi, 0, 0)),
            _resident(w_pool.shape, lambda bi, mi: (0, 0, 0)),
            _resident((1, c), lambda bi, mi: (0, 0)),
        ],
        out_specs=pl.BlockSpec((None, tm, c), lambda bi, mi: (bi, mi, 0)),
        out_shape=jax.ShapeDtypeStruct((b, t, c), BF16),
        scratch_shapes=[pltpu.VMEM((POOL_HALO + tm, c), F32)],
        compiler_params=_cparams(("arbitrary", "arbitrary")),
        name="pool_mixer",
    )(u, prefix, w_pool, scale)


def _fox_update(s, v, m_ref, l_ref, acc_ref, tk):
    m_prev = m_ref[...]
    m_new = jnp.maximum(m_prev, jnp.max(s, axis=1, keepdims=True))
    alpha = jnp.exp(m_prev - m_new)
    p = jnp.exp(s - _tile_lanes(m_new, tk))
    l_ref[...] = alpha * l_ref[...] + jnp.sum(p, axis=1, keepdims=True)
    acc_ref[...] = alpha * acc_ref[...] + _dot(p.astype(BF16), v)
    m_ref[...] = m_new


def _fox_prompt_kernel(q_ref, k_ref, v_ref, cq_ref, ck_ref, o_ref, m_ref, l_ref, acc_ref, *, tq, scale):
    qi = pl.program_id(1)
    m_ref[...] = jnp.full_like(m_ref, -jnp.inf)
    l_ref[...] = jnp.zeros_like(l_ref)
    acc_ref[...] = jnp.zeros_like(acc_ref)
    q = q_ref[...]
    cq = _tile_lanes(cq_ref[...], tq)

    def scores(j):
        rows = pl.ds(pl.multiple_of(j * tq, tq), tq)
        s = _dot_nt(q, k_ref[rows, :]) * scale + cq - ck_ref[j]
        return s, v_ref[rows, :]

    def body(j, carry):
        s, v = scores(j)
        _fox_update(s, v, m_ref, l_ref, acc_ref, tq)
        return carry

    lax.fori_loop(0, qi, body, 0)
    s, v = scores(qi)
    row = lax.broadcasted_iota(jnp.int32, (tq, tq), 0)
    col = lax.broadcasted_iota(jnp.int32, (tq, tq), 1)
    s = jnp.where(col <= row, s, -jnp.inf)
    _fox_update(s, v, m_ref, l_ref, acc_ref, tq)
    o_ref[...] = (acc_ref[...] / l_ref[...]).astype(o_ref.dtype)


def fox_prompt(q, k, v, cq, ck, tq=512):
    t, hd = q.shape
    h = hd // HEAD_DIM
    tq = min(tq, t)
    nq = t // tq
    return pl.pallas_call(
        functools.partial(_fox_prompt_kernel, tq=tq, scale=HEAD_DIM ** -0.5),
        grid=(h, nq),
        in_specs=[
            pl.BlockSpec((tq, HEAD_DIM), lambda hi, qi: (qi, hi)),
            pl.BlockSpec((t, HEAD_DIM), lambda hi, qi: (0, hi)),
            pl.BlockSpec((t, HEAD_DIM), lambda hi, qi: (0, hi)),
            pl.BlockSpec((None, tq, LANES), lambda hi, qi: (hi, qi, 0)),
            pl.BlockSpec((None, nq, 1, tq), lambda hi, qi: (hi, 0, 0, 0)),
        ],
        out_specs=pl.BlockSpec((tq, HEAD_DIM), lambda hi, qi: (qi, hi)),
        out_shape=jax.ShapeDtypeStruct((t, hd), BF16),
        scratch_shapes=[pltpu.VMEM((tq, LANES), F32), pltpu.VMEM((tq, LANES), F32),
                        pltpu.VMEM((tq, HEAD_DIM), F32)],
        compiler_params=_cparams(("parallel", "arbitrary")),
        name="fox_prompt",
    )(q, k, v, cq, ck)


def _fox_decode_kernel(q_ref, kn_ref, vn_ref, kc_ref, vc_ref, cq_ref, ckc_ref, ckn_ref, o_ref, *, scale):
    q = q_ref[...]
    tq = q.shape[0]
    p_len = kc_ref.shape[0]
    n_new = kn_ref.shape[0]
    cq = cq_ref[...]
    s_c = _dot_nt(q, kc_ref[...].astype(BF16)) * scale + _tile_lanes(cq, p_len) - ckc_ref[...]
    s_n = _dot_nt(q, kn_ref[...]) * scale + _tile_lanes(cq, n_new) - ckn_ref[...]
    row = lax.broadcasted_iota(jnp.int32, (tq, n_new), 0)
    col = lax.broadcasted_iota(jnp.int32, (tq, n_new), 1)
    s_n = jnp.where(col <= row, s_n, -jnp.inf)
    m = jnp.maximum(jnp.max(s_c, axis=1, keepdims=True), jnp.max(s_n, axis=1, keepdims=True))
    p_c = jnp.exp(s_c - m)
    p_n = jnp.exp(s_n - m)
    l = jnp.sum(p_c, axis=1, keepdims=True) + jnp.sum(p_n, axis=1, keepdims=True)
    acc = _dot(p_c.astype(BF16), vc_ref[...].astype(BF16)) + _dot(p_n.astype(BF16), vn_ref[...])
    o_ref[...] = (acc / l).astype(o_ref.dtype)


def fox_decode(q, k_new, v_new, k_cache, v_cache, cq, ck_cache, ck_new):
    b, tq, hd = q.shape
    h = hd // HEAD_DIM
    n_new = k_new.shape[1]
    p_len = k_cache.shape[1]
    return pl.pallas_call(
        functools.partial(_fox_decode_kernel, scale=HEAD_DIM ** -0.5),
        grid=(b, h),
        in_specs=[
            pl.BlockSpec((None, tq, HEAD_DIM), lambda bi, hi: (bi, 0, hi)),
            pl.BlockSpec((None, n_new, HEAD_DIM), lambda bi, hi: (bi, 0, hi)),
            pl.BlockSpec((None, n_new, HEAD_DIM), lambda bi, hi: (bi, 0, hi)),
            pl.BlockSpec((None, p_len, HEAD_DIM), lambda bi, hi: (bi, 0, hi)),
            pl.BlockSpec((None, p_len, HEAD_DIM), lambda bi, hi: (bi, 0, hi)),
            pl.BlockSpec((None, None, tq, LANES), lambda bi, hi: (bi, hi, 0, 0)),
            pl.BlockSpec((None, None, 1, p_len), lambda bi, hi: (bi, hi, 0, 0)),
            pl.BlockSpec((None, None, 1, n_new), lambda bi, hi: (bi, hi, 0, 0)),
        ],
        out_specs=pl.BlockSpec((None, tq, HEAD_DIM), lambda bi, hi: (bi, 0, hi)),
        out_shape=jax.ShapeDtypeStruct((b, tq, hd), BF16),
        compiler_params=_cparams(("parallel", "parallel")),
        name="fox_decode",
    )(q, k_new, v_new, k_cache, v_cache, cq, ck_cache, ck_new)


def _suffix_sums(lr, tri, carry):
    width = tri.shape[0]
    n = lr.shape[1]
    parts = []
    for c0 in range(n - width, -1, -width):
        piece = lr[:, c0:c0 + width]
        hi = piece.astype(BF16)
        lo = (piece - hi.astype(F32)).astype(BF16)
        c = _dot(hi, tri) + _dot(lo, tri) + carry
        carry = c[:, 0:1]
        parts.append(c)
    parts.reverse()
    return (parts[0] if len(parts) == 1 else jnp.concatenate(parts, axis=1)), carry


def _sb_block(z, v, tri, carry, mask):
    lr = -_softplus(z)
    if mask is not None:
        lr = jnp.where(mask, lr, 0.0)
    c, new_carry = _suffix_sums(lr, tri, carry)
    a = jnp.exp(z + c)
    if mask is not None:
        a = jnp.where(mask, a, 0.0)
    return _dot(a.astype(BF16), v), new_carry


def _sb_prompt_kernel(q_ref, k_ref, v_ref, tri_ref, o_ref, r_ref, acc_ref, *, tq, scale):
    qi = pl.program_id(1)
    q = q_ref[...]
    tri = tri_ref[...]

    def block(j, carry, mask):
        rows = pl.ds(pl.multiple_of(j * tq, tq), tq)
        z = _dot_nt(q, k_ref[rows, :]) * scale
        return _sb_block(z, v_ref[rows, :], tri, carry, mask)

    row = lax.broadcasted_iota(jnp.int32, (tq, tq), 0)
    col = lax.broadcasted_iota(jnp.int32, (tq, tq), 1)
    out, carry = block(qi, jnp.zeros((tq, 1), F32), col < row)
    acc_ref[...] = out
    r_ref[...] = jnp.broadcast_to(carry, r_ref.shape)

    def body(i, c):
        out, carry = block(qi - 1 - i, r_ref[:, 0:1], None)
        acc_ref[...] += out
        r_ref[...] = jnp.broadcast_to(carry, r_ref.shape)
        return c

    lax.fori_loop(0, qi, body, 0)
    o_ref[...] = acc_ref[...].astype(o_ref.dtype)


def sb_prompt(q, k, v, tri, tq=512):
    t, hd = q.shape
    h = hd // HEAD_DIM
    tq = min(tq, t)
    return pl.pallas_call(
        functools.partial(_sb_prompt_kernel, tq=tq, scale=HEAD_DIM ** -0.5),
        grid=(h, t // tq),
        in_specs=[
            pl.BlockSpec((tq, HEAD_DIM), lambda hi, qi: (qi, hi)),
            pl.BlockSpec((t, HEAD_DIM), lambda hi, qi: (0, hi)),
            pl.BlockSpec((t, HEAD_DIM), lambda hi, qi: (0, hi)),
            _resident(tri.shape, lambda hi, qi: (0, 0)),
        ],
        out_specs=pl.BlockSpec((tq, HEAD_DIM), lambda hi, qi: (qi, hi)),
        out_shape=jax.ShapeDtypeStruct((t, hd), BF16),
        scratch_shapes=[pltpu.VMEM((tq, LANES), F32), pltpu.VMEM((tq, HEAD_DIM), F32)],
        compiler_params=_cparams(("parallel", "arbitrary")),
        name="sb_prompt",
    )(q, k, v, tri)


def _sb_decode_kernel(q_ref, kn_ref, vn_ref, kc_ref, vc_ref, tri_ref, o_ref, *, scale):
    q = q_ref[...]
    tq = q.shape[0]
    n_new = kn_ref.shape[0]
    n_chunks = kc_ref.shape[0] // TRI
    tri = tri_ref[...]
    row = lax.broadcasted_iota(jnp.int32, (tq, n_new), 0)
    col = lax.broadcasted_iota(jnp.int32, (tq, n_new), 1)
    z = _dot_nt(q, kn_ref[...]) * scale
    acc, carry = _sb_block(z, vn_ref[...], tri_ref[0:n_new, 0:n_new], jnp.zeros((tq, 1), F32), col < row)

    def body(i, state):
        acc, carry = state
        rows = pl.ds(pl.multiple_of((n_chunks - 1 - i) * TRI, TRI), TRI)
        z = _dot_nt(q, kc_ref[rows, :].astype(BF16)) * scale
        out, carry = _sb_block(z, vc_ref[rows, :].astype(BF16), tri, carry, None)
        return acc + out, carry

    acc, _ = lax.fori_loop(0, n_chunks, body, (acc, carry))
    o_ref[...] = acc.astype(o_ref.dtype)


def sb_decode(q, k_new, v_new, k_cache, v_cache, tri):
    b, tq, hd = q.shape
    h = hd // HEAD_DIM
    n_new = k_new.shape[1]
    p_len = k_cache.shape[1]
    return pl.pallas_call(
        functools.partial(_sb_decode_kernel, scale=HEAD_DIM ** -0.5),
        grid=(b, h),
        in_specs=[
            pl.BlockSpec((None, tq, HEAD_DIM), lambda bi, hi: (bi, 0, hi)),
            pl.BlockSpec((None, n_new, HEAD_DIM), lambda bi, hi: (bi, 0, hi)),
            pl.BlockSpec((None, n_new, HEAD_DIM), lambda bi, hi: (bi, 0, hi)),
            pl.BlockSpec((None, p_len, HEAD_DIM), lambda bi, hi: (bi, 0, hi)),
            pl.BlockSpec((None, p_len, HEAD_DIM), lambda bi, hi: (bi, 0, hi)),
            _resident(tri.shape, lambda bi, hi: (0, 0)),
        ],
        out_specs=pl.BlockSpec((None, tq, HEAD_DIM), lambda bi, hi: (bi, 0, hi)),
        out_shape=jax.ShapeDtypeStruct((b, tq, hd), BF16),
        compiler_params=_cparams(("parallel", "parallel")),
        name="sb_decode",
    )(q, k_new, v_new, k_cache, v_cache, tri)


def _out_proj_kernel(xa_ref, xb_ref, w_ref, h_ref, o_ref):
    ka = xa_ref.shape[1]
    o_ref[...] = h_ref[...] + _dot(xa_ref[...], w_ref[:ka, :]) + _dot(xb_ref[...], w_ref[ka:, :])


def out_proj(xa, xb, col_b, w, h, tm=512):
    m, d = h.shape
    ka = w.shape[0] // 2
    tm = min(tm, m)
    return pl.pallas_call(
        _out_proj_kernel,
        grid=(m // tm,),
        in_specs=[
            pl.BlockSpec((tm, ka), lambda i: (i, 0)),
            pl.BlockSpec((tm, ka), lambda i: (i, col_b)),
            _resident(w.shape, lambda i: (0, 0)),
            pl.BlockSpec((tm, d), lambda i: (i, 0)),
        ],
        out_specs=pl.BlockSpec((tm, d), lambda i: (i, 0)),
        out_shape=jax.ShapeDtypeStruct((m, d), F32),
        compiler_params=_cparams(("parallel",)),
        name="out_proj",
    )(xa, xb, w, h)


def _ffn_kernel(h_ref, g_ref, wg_ref, wu_ref, wd_ref, gf_ref, o_ref, xn_ref, *, final_norm):
    fi = pl.program_id(1)

    @pl.when(fi == 0)
    def _():
        xn_ref[...] = _rms(h_ref[...], g_ref[...]).astype(BF16)
        o_ref[...] = jnp.zeros_like(o_ref)

    xn = xn_ref[...]
    gate = _dot(xn, wg_ref[...])
    up = _dot(xn, wu_ref[...])
    mid = (gate / (1.0 + jnp.exp(-gate)) * up).astype(BF16)
    width = wd_ref.shape[0]
    for c0 in range(0, o_ref.shape[1], width):
        o_ref[:, c0:c0 + width] += _dot(mid, wd_ref[:, c0:c0 + width])

    @pl.when(fi == pl.num_programs(1) - 1)
    def _():
        hn = h_ref[...] + o_ref[...]
        o_ref[...] = _rms(hn, gf_ref[...]) if final_norm else hn


def ffn(h, g, w_gate, w_up, w_down, g_final, final_norm, tm=1024, tf=512):
    m, d = h.shape
    f = w_gate.shape[1]
    tm = min(tm, m)
    return pl.pallas_call(
        functools.partial(_ffn_kernel, final_norm=final_norm),
        grid=(m // tm, f // tf),
        in_specs=[
            pl.BlockSpec((tm, d), lambda i, j: (i, 0), pipeline_mode=pl.Buffered(1)),
            _resident((1, d), lambda i, j: (0, 0)),
            pl.BlockSpec((d, tf), lambda i, j: (0, j)),
            pl.BlockSpec((d, tf), lambda i, j: (0, j)),
            pl.BlockSpec((tf, d), lambda i, j: (j, 0)),
            _resident((1, d), lambda i, j: (0, 0)),
        ],
        out_specs=pl.BlockSpec((tm, d), lambda i, j: (i, 0)),
        out_shape=jax.ShapeDtypeStruct((m, d), F32),
        scratch_shapes=[pltpu.VMEM((tm, d), BF16)],
        compiler_params=_cparams(("parallel", "arbitrary")),
        name="ffn",
    )(h, g, w_gate, w_up, w_down, g_final)


def _pad_rows(x, rows):
    return jnp.pad(x, ((0, 0), (0, rows - x.shape[1]), (0, 0)))


def _tri_matrix():
    idx = np.arange(TRI)
    return jnp.asarray(idx[:, None] >= idx[None, :], dtype=BF16)


def _layer0_groups(pool_w, fox_w):
    outs = [("f32",), ("bf16",), ("f32", "bf16"), ("f32", "bf16")]
    groups, out_defs, col = [], [], 0
    for kinds in outs:
        width = pool_w if col == 0 else fox_w
        o = []
        for kind in kinds:
            o.append((len(out_defs), 0, kind))
            out_defs.append((width, F32 if kind == "f32" else BF16))
        groups.append((col, width, tuple(o)))
        col += width
    groups.append((col, LANES, ((len(out_defs), 0, "logsig"),)))
    out_defs.append((LANES, F32))
    return tuple(groups), out_defs


def _layer1_groups(d):
    half = d // 2
    groups = []
    for part, outs in enumerate([(0,), (1, 2), (3, 4)]):
        for c in range(2):
            kinds = ("bf16",) if part == 0 else ("f32", "bf16")
            groups.append((part * d + c * half, half,
                           tuple((oi, c * half, kind) for oi, kind in zip(outs, kinds))))
    out_defs = [(d, BF16), (d, F32), (d, BF16), (d, F32), (d, BF16)]
    return tuple(groups), out_defs


def _run_trunk(x, n_past, pool_c, fk_c, fv_c, flf_c, sk_c, sv_c, wts):
    b, t, d = x.shape
    m = b * t
    h = x.reshape(m, d)
    pool_w = wts["w_pool"].shape[0] * wts["w_pool"].shape[1]
    fox_w = d - pool_w
    fox_h = fox_w // HEAD_DIM
    sb_h = d // HEAD_DIM
    first = fk_c is None
    zero_bias = jnp.zeros((1, LANES), F32)

    groups, out_defs = _layer0_groups(pool_w, fox_w)
    u, q, k, kb, v, vb, lf = norm_proj(h, wts["ln_mix0"], wts["w_in"], wts["b_forget"], groups, out_defs, tm=256)
    log_f = lf[:, :fox_h]
    if first:
        prefix = jnp.zeros((b, POOL_HALO, pool_w), F32)
    else:
        prefix = jnp.pad(pool_c, ((0, 0), (POOL_HALO - POOL_STATE, 0), (0, 0)))
    y_pool = pool_mixer(u.reshape(b, t, pool_w), prefix, wts["w_pool"], wts["pool_scale"], n_past)
    if t >= POOL_STATE:
        pool_state = u.reshape(b, t, pool_w)[:, t - POOL_STATE:]
    else:
        pool_state = jnp.concatenate([prefix[:, 1:], u.reshape(b, t, pool_w)], axis=1)[:, -POOL_STATE:]

    if first:
        tq = min(512, t)
        cum = cumsum_rows(lf)[:, :fox_h]
        cq = jnp.broadcast_to(cum.T[:, :, None], (fox_h, t, LANES))
        ck = cum.T.reshape(fox_h, t // tq, 1, tq)
        o = fox_prompt(q, kb, vb, cq, ck, tq=tq)
    else:
        n_new = LANES
        lf_new = log_f.reshape(b, t, fox_h).transpose(1, 0, 2).reshape(t, b * fox_h)
        lf_all = jnp.concatenate([flf_c.transpose(1, 0, 2).reshape(n_past, b * fox_h), lf_new], axis=0)
        rows = n_past + t
        rows_pad = -(-rows // 512) * 512
        lf_all = jnp.pad(lf_all, ((0, rows_pad - rows), (0, LANES - b * fox_h)))
        cum = cumsum_rows(lf_all)[:rows, :b * fox_h].reshape(rows, b, fox_h).transpose(1, 2, 0)
        ck_cache = cum[:, :, None, :n_past]
        cum_new = cum[:, :, n_past:]
        ck_new = jnp.pad(cum_new, ((0, 0), (0, 0), (0, n_new - t)))[:, :, None, :]
        cq = jnp.broadcast_to(cum_new[:, :, :, None], (b, fox_h, t, LANES))
        o = fox_decode(q.reshape(b, t, fox_w), _pad_rows(kb.reshape(b, t, fox_w), n_new),
                       _pad_rows(vb.reshape(b, t, fox_w), n_new), fk_c, fv_c, cq, ck_cache, ck_new)
        o = o.reshape(m, fox_w)
    h = out_proj(y_pool.reshape(m, pool_w), o, 0, wts["w_out_ab"], h)
    h = ffn(h, wts["ln_ffn0"], wts["w_gate0"], wts["w_up0"], wts["w_down0"], wts["ln_final"], False)

    groups1, out_defs1 = _layer1_groups(d)
    q1, k1, k1b, v1, v1b = norm_proj(h, wts["ln_mix1"], wts["w_qkv"], zero_bias, groups1, out_defs1, tm=256)
    tri = _tri_matrix()
    if first:
        o1 = sb_prompt(q1, k1b, v1b, tri)
    else:
        n_new = LANES
        o1 = sb_decode(q1.reshape(b, t, d), _pad_rows(k1b.reshape(b, t, d), n_new),
                       _pad_rows(v1b.reshape(b, t, d), n_new), sk_c, sv_c, tri).reshape(m, d)
    h = out_proj(o1, o1, 1, wts["w_out_sb"], h)
    y = ffn(h, wts["ln_ffn1"], wts["w_gate1"], wts["w_up1"], wts["w_down1"], wts["ln_final"], True)

    return (y.reshape(b, t, d), pool_state[None], k.reshape(1, b, t, fox_h, HEAD_DIM),
            v.reshape(1, b, t, fox_h, HEAD_DIM), log_f.reshape(1, b, t, fox_h),
            k1.reshape(1, b, t, sb_h, HEAD_DIM), v1.reshape(1, b, t, sb_h, HEAD_DIM))


def kernel(x_prompt, x_sample, cache_pool, cache_fox_k, cache_fox_v, cache_fox_logf, cache_sb_k, cache_sb_v, ln_mix, w_in_ab, b_forget, w_pool, pool_scale, w_out_ab, w_qkv_sb, w_out_sb, ln_ffn, w_gate, w_up, w_down, ln_final):
    fox_h = b_forget.shape[-1]
    w_in = jnp.pad(w_in_ab[0], ((0, 0), (0, LANES - fox_h))).astype(BF16)
    wts = {
        "ln_mix0": ln_mix[0][None], "ln_mix1": ln_mix[1][None],
        "ln_ffn0": ln_ffn[0][None], "ln_ffn1": ln_ffn[1][None], "ln_final": ln_final[None],
        "w_in": w_in,
        "b_forget": jnp.pad(b_forget[0], (0, LANES - fox_h))[None],
        "w_pool": w_pool[0].astype(BF16), "pool_scale": pool_scale[0][None],
        "w_out_ab": w_out_ab[0].astype(BF16), "w_qkv": w_qkv_sb[0].astype(BF16),
        "w_out_sb": w_out_sb[0].astype(BF16),
        "w_gate0": w_gate[0].astype(BF16), "w_up0": w_up[0].astype(BF16), "w_down0": w_down[0].astype(BF16),
        "w_gate1": w_gate[1].astype(BF16), "w_up1": w_up[1].astype(BF16), "w_down1": w_down[1].astype(BF16),
    }
    out_p = _run_trunk(x_prompt, 0, None, None, None, None, None, None, wts)
    n_past = cache_fox_k.shape[2]
    bs = x_sample.shape[0]
    out_s = _run_trunk(x_sample, n_past, cache_pool[0], cache_fox_k[0].reshape(bs, n_past, -1),
                       cache_fox_v[0].reshape(bs, n_past, -1), cache_fox_logf[0],
                       cache_sb_k[0].reshape(bs, n_past, -1), cache_sb_v[0].reshape(bs, n_past, -1), wts)
    return (out_p[0], out_s[0]) + out_p[1:] + out_s[1:]
```

```python
import functools

import numpy as np
import jax
import jax.numpy as jnp
from jax import lax
from jax.experimental import pallas as pl
from jax.experimental.pallas import tpu as pltpu

HEAD_DIM = 128
EPS = 1e-6
POOL_WINDOWS = (2, 4, 8, 16)
POOL_STATE = max(POOL_WINDOWS) - 1
POOL_HALO = 16
LANES = 128
TRI = 256
VMEM_LIMIT = 56 * 1024 * 1024
LOG2E = 1.4426950408889634
QK_SCALE2 = HEAD_DIM ** -0.5 * LOG2E

F32 = jnp.float32
BF16 = jnp.bfloat16


def _cparams(sem, vmem=VMEM_LIMIT):
    return pltpu.CompilerParams(dimension_semantics=sem, vmem_limit_bytes=vmem)


def _resident(shape, index_map):
    return pl.BlockSpec(shape, index_map, pipeline_mode=pl.Buffered(1))


def _dot(a, b):
    return jnp.dot(a, b, preferred_element_type=F32)


def _dot_nt(a, b):
    return lax.dot_general(a, b, (((1,), (1,)), ((), ())), preferred_element_type=F32)


def _rms(x, g):
    ms = jnp.mean(x * x, axis=-1, keepdims=True)
    return x * lax.rsqrt(ms + EPS) * g


def _softplus(z):
    return jnp.maximum(z, 0.0) + jnp.log1p(jnp.exp(-jnp.abs(z)))


def _softplus2(z2):
    return jnp.maximum(z2, 0.0) + jnp.log(1.0 + jnp.exp2(-jnp.abs(z2))) * LOG2E


def _tile_lanes(x, n):
    reps = n // LANES
    return x if reps == 1 else jnp.concatenate([x] * reps, axis=1)


def _head_cols(h):
    return slice(h * HEAD_DIM, (h + 1) * HEAD_DIM)


def _head_rows(ref, h, heads, n):
    return ref[pl.ds(h, n, stride=heads), :]


def _norm_proj_kernel(x_ref, g_ref, w_ref, b_ref, *out_refs, groups):
    xn = _rms(x_ref[...], g_ref[...]).astype(BF16)
    for w_col, width, outs in groups:
        acc = _dot(xn, w_ref[:, w_col:w_col + width])
        for out_idx, out_col, kind in outs:
            if kind == "logsig":
                val = -_softplus(-(acc + b_ref[...]))
            elif kind == "qscaled":
                val = acc * QK_SCALE2
            else:
                val = acc
            o = out_refs[out_idx]
            o[:, out_col:out_col + width] = val.astype(o.dtype)


def norm_proj(x, g, w, bias, groups, out_defs, tm):
    m, d = x.shape
    n = w.shape[1]
    tm = min(tm, m)
    out_shape = [jax.ShapeDtypeStruct((m, c), dt) for c, dt in out_defs]
    out_specs = [pl.BlockSpec((tm, c), lambda i: (i, 0)) for c, _ in out_defs]
    return pl.pallas_call(
        functools.partial(_norm_proj_kernel, groups=groups),
        grid=(m // tm,),
        in_specs=[
            pl.BlockSpec((tm, d), lambda i: (i, 0)),
            _resident((1, d), lambda i: (0, 0)),
            _resident((d, n), lambda i: (0, 0)),
            _resident((1, LANES), lambda i: (0, 0)),
        ],
        out_specs=out_specs,
        out_shape=out_shape,
        compiler_params=_cparams(("parallel",)),
        name="norm_proj",
    )(x, g, w, bias)


def _cumsum_kernel(x_ref, o_ref, carry_ref, *, tb):
    @pl.when(pl.program_id(0) == 0)
    def _():
        carry_ref[...] = jnp.zeros_like(carry_ref)

    x = x_ref[...]
    row = lax.broadcasted_iota(jnp.int32, (tb, tb), 0)
    col = lax.broadcasted_iota(jnp.int32, (tb, tb), 1)
    low = jnp.where(col <= row, 1.0, 0.0).astype(BF16)
    hi = x.astype(BF16)
    r1 = x - hi.astype(F32)
    mid = r1.astype(BF16)
    lo = (r1 - mid.astype(F32)).astype(BF16)
    out = _dot(low, hi) + _dot(low, mid) + _dot(low, lo) + carry_ref[0:1, :]
    o_ref[...] = out
    carry_ref[...] = jnp.broadcast_to(out[tb - 1:tb, :], carry_ref.shape)


def cumsum_rows(x, tb=512):
    t, c = x.shape
    tb = min(tb, t)
    return pl.pallas_call(
        functools.partial(_cumsum_kernel, tb=tb),
        grid=(t // tb,),
        in_specs=[pl.BlockSpec((tb, c), lambda i: (i, 0))],
        out_specs=pl.BlockSpec((tb, c), lambda i: (i, 0)),
        out_shape=jax.ShapeDtypeStruct((t, c), F32),
        scratch_shapes=[pltpu.VMEM((8, c), F32)],
        compiler_params=_cparams(("arbitrary",)),
        name="cumsum_rows",
    )(x)


def _pool_kernel(u_ref, pre_ref, w_ref, sc_ref, o_ref, buf_ref, *, tm, pos0, group):
    mi = pl.program_id(1)

    @pl.when(mi == 0)
    def _():
        buf_ref[0:POOL_HALO, :] = pre_ref[...]

    @pl.when(mi > 0)
    def _():
        buf_ref[0:POOL_HALO, :] = buf_ref[tm:tm + POOL_HALO, :]

    buf_ref[POOL_HALO:POOL_HALO + tm, :] = u_ref[...]
    pos = pos0 + mi * tm + lax.broadcasted_iota(jnp.int32, (tm, 1), 0)
    for gi, win in enumerate(POOL_WINDOWS):
        cs = slice(gi * group, (gi + 1) * group)
        tot = buf_ref[POOL_HALO:POOL_HALO + tm, cs]
        for back in range(1, win):
            tot = tot + buf_ref[POOL_HALO - back:POOL_HALO - back + tm, cs]
        count = jnp.minimum(pos + 1, win).astype(F32)
        diff = tot / count - u_ref[:, cs]
        y = _dot(diff.astype(BF16), w_ref[gi]) * sc_ref[:, cs]
        o_ref[:, cs] = y.astype(o_ref.dtype)


def pool_mixer(u, prefix, w_pool, scale, pos0, tm=512):
    b, t, c = u.shape
    tm = min(tm, t)
    group = c // len(POOL_WINDOWS)
    return pl.pallas_call(
        functools.partial(_pool_kernel, tm=tm, pos0=pos0, group=group),
        grid=(b, t // tm),
        in_specs=[
            pl.BlockSpec((None, tm, c), lambda bi, mi: (bi, mi, 0)),
            pl.BlockSpec((None, POOL_HALO, c), lambda bi, mi: (bi, 0, 0)),
            _resident(w_pool.shape, lambda bi, mi: (0, 0, 0)),
            _resident((1, c), lambda bi, mi: (0, 0)),
        ],
        out_specs=pl.BlockSpec((None, tm, c), lambda bi, mi: (bi, mi, 0)),
        out_shape=jax.ShapeDtypeStruct((b, t, c), BF16),
        scratch_shapes=[pltpu.VMEM((POOL_HALO + tm, c), F32)],
        compiler_params=_cparams(("arbitrary", "arbitrary")),
        name="pool_mixer",
    )(u, prefix, w_pool, scale)


def _fox_update(s2, v, m_prev, l_prev, acc_prev):
    m_new = jnp.maximum(m_prev, jnp.max(s2, axis=1, keepdims=True))
    alpha = jnp.exp2(m_prev - m_new)
    p = jnp.exp2(s2 - _tile_lanes(m_new, s2.shape[1]))
    l_new = alpha * l_prev + jnp.sum(p, axis=1, keepdims=True)
    acc_new = alpha * acc_prev + _dot(p.astype(BF16), v)
    return m_new, l_new, acc_new


def _fox_prompt_kernel(q_ref, k_ref, v_ref, cq_ref, ck_ref, o_ref, m_ref, l_ref, acc_ref, *, tq):
    qi = pl.program_id(1)
    m_ref[...] = jnp.full_like(m_ref, -jnp.inf)
    l_ref[...] = jnp.zeros_like(l_ref)
    acc_ref[...] = jnp.zeros_like(acc_ref)
    q = q_ref[...]

    def update(j, nblk, causal):
        width = nblk * tq
        rows = pl.ds(pl.multiple_of(j * tq, tq), width)
        ck = ck_ref[j] if nblk == 1 else jnp.concatenate([ck_ref[j + i] for i in range(nblk)], axis=1)
        s2 = _dot_nt(q, k_ref[rows, :]) + _tile_lanes(cq_ref[...], width) - ck
        if causal:
            row = lax.broadcasted_iota(jnp.int32, (tq, width), 0)
            col = lax.broadcasted_iota(jnp.int32, (tq, width), 1)
            s2 = jnp.where(col <= row, s2, -jnp.inf)
        m_ref[...], l_ref[...], acc_ref[...] = _fox_update(s2, v_ref[rows, :], m_ref[...], l_ref[...], acc_ref[...])

    def body(i, carry):
        update(4 * i, 4, False)
        return carry

    lax.fori_loop(0, qi // 4, body, 0)

    @pl.when(qi % 4 >= 2)
    def _():
        update(qi - qi % 4, 2, False)

    @pl.when(qi % 2 == 1)
    def _():
        update(qi - 1, 1, False)

    update(qi, 1, True)
    o_ref[...] = (acc_ref[...] / l_ref[...]).astype(o_ref.dtype)


def fox_prompt(q, k, v, cq, ck, tq=512):
    t, hd = q.shape
    h = hd // HEAD_DIM
    tq = min(tq, t)
    nq = t // tq
    return pl.pallas_call(
        functools.partial(_fox_prompt_kernel, tq=tq),
        grid=(h, nq),
        in_specs=[
            pl.BlockSpec((tq, HEAD_DIM), lambda hi, qi: (qi, hi)),
            pl.BlockSpec((t, HEAD_DIM), lambda hi, qi: (0, hi)),
            pl.BlockSpec((t, HEAD_DIM), lambda hi, qi: (0, hi)),
            pl.BlockSpec((None, tq, LANES), lambda hi, qi: (hi, qi, 0)),
            pl.BlockSpec((None, nq, 1, tq), lambda hi, qi: (hi, 0, 0, 0)),
        ],
        out_specs=pl.BlockSpec((tq, HEAD_DIM), lambda hi, qi: (qi, hi)),
        out_shape=jax.ShapeDtypeStruct((t, hd), BF16),
        scratch_shapes=[pltpu.VMEM((tq, LANES), F32), pltpu.VMEM((tq, LANES), F32),
                        pltpu.VMEM((tq, HEAD_DIM), F32)],
        compiler_params=_cparams(("parallel", "arbitrary")),
        name="fox_prompt",
    )(q, k, v, cq, ck)


def _fox_decode_kernel(q_ref, kn_ref, vn_ref, kc_ref, vc_ref, cq_ref, ckc_ref, ckn_ref, o_ref,
                       m_ref, l_ref, acc_ref, *, heads, tk):
    kb = pl.program_id(1)
    tq = q_ref.shape[0]
    n_new = kn_ref.shape[0]

    @pl.when(kb == 0)
    def _():
        row = lax.broadcasted_iota(jnp.int32, (tq, n_new), 0)
        col = lax.broadcasted_iota(jnp.int32, (tq, n_new), 1)
        for h in range(heads):
            hc = _head_cols(h)
            s2 = _dot_nt(q_ref[:, hc], kn_ref[:, hc]) + _tile_lanes(cq_ref[h], n_new) - ckn_ref[h]
            s2 = jnp.where(col <= row, s2, -jnp.inf)
            m = jnp.max(s2, axis=1, keepdims=True)
            p = jnp.exp2(s2 - m)
            m_ref[h] = jnp.broadcast_to(m, (tq, LANES))
            l_ref[h] = jnp.broadcast_to(jnp.sum(p, axis=1, keepdims=True), (tq, LANES))
            acc_ref[h] = _dot(p.astype(BF16), vn_ref[:, hc])

    for h in range(heads):
        k_h = _head_rows(kc_ref, h, heads, tk).astype(BF16)
        v_h = _head_rows(vc_ref, h, heads, tk).astype(BF16)
        s2 = _dot_nt(q_ref[:, _head_cols(h)], k_h) + _tile_lanes(cq_ref[h], tk) - ckc_ref[h]
        m_ref[h], l_ref[h], acc_ref[h] = _fox_update(s2, v_h, m_ref[h], l_ref[h], acc_ref[h])

    @pl.when(kb == pl.num_programs(1) - 1)
    def _():
        for h in range(heads):
            o_ref[:, _head_cols(h)] = (acc_ref[h] / l_ref[h]).astype(o_ref.dtype)


def fox_decode(q, k_new, v_new, k_cache, v_cache, cq, ck_cache, ck_new, heads, tk=1024):
    b, tq, hd = q.shape
    n_new = k_new.shape[1]
    p_len = k_cache.shape[1] // heads
    tk = min(tk, p_len)
    return pl.pallas_call(
        functools.partial(_fox_decode_kernel, heads=heads, tk=tk),
        grid=(b, p_len // tk),
        in_specs=[
            pl.BlockSpec((None, tq, hd), lambda bi, kb: (bi, 0, 0)),
            pl.BlockSpec((None, n_new, hd), lambda bi, kb: (bi, 0, 0)),
            pl.BlockSpec((None, n_new, hd), lambda bi, kb: (bi, 0, 0)),
            pl.BlockSpec((None, tk * heads, HEAD_DIM), lambda bi, kb: (bi, kb, 0)),
            pl.BlockSpec((None, tk * heads, HEAD_DIM), lambda bi, kb: (bi, kb, 0)),
            pl.BlockSpec((None, heads, tq, LANES), lambda bi, kb: (bi, 0, 0, 0)),
            pl.BlockSpec((None, heads, 1, tk), lambda bi, kb: (bi, 0, 0, kb)),
            pl.BlockSpec((None, heads, 1, n_new), lambda bi, kb: (bi, 0, 0, 0)),
        ],
        out_specs=pl.BlockSpec((None, tq, hd), lambda bi, kb: (bi, 0, 0)),
        out_shape=jax.ShapeDtypeStruct((b, tq, hd), BF16),
        scratch_shapes=[pltpu.VMEM((heads, tq, LANES), F32), pltpu.VMEM((heads, tq, LANES), F32),
                        pltpu.VMEM((heads, tq, HEAD_DIM), F32)],
        compiler_params=_cparams(("parallel", "arbitrary")),
        name="fox_decode",
    )(q, k_new, v_new, k_cache, v_cache, cq, ck_cache, ck_new)


def _suffix_sums(sp, ntri, carry):
    width = ntri.shape[0]
    n = sp.shape[1]
    parts = []
    for c0 in range(n - width, -1, -width):
        c = _dot(sp[:, c0:c0 + width].astype(BF16), ntri) + carry
        carry = c[:, 0:1]
        parts.append(c)
    parts.reverse()
    return (parts[0] if len(parts) == 1 else jnp.concatenate(parts, axis=1)), carry


def _sb_weights(z2, ntri, carry, mask):
    sp = _softplus2(z2)
    if mask is not None:
        sp = jnp.where(mask, sp, 0.0)
    c, new_carry = _suffix_sums(sp, ntri, carry)
    a = jnp.exp2(z2 + c)
    if mask is not None:
        a = jnp.where(mask, a, 0.0)
    return a.astype(BF16), new_carry


def _sb_prompt_kernel(q_ref, k_ref, v_ref, ntri_ref, o_ref, r_ref, acc_ref, *, tq):
    qi = pl.program_id(1)
    q = q_ref[...]
    ntri = ntri_ref[...]

    def block(j, nblk, carry, mask):
        rows = pl.ds(pl.multiple_of(j * tq, tq), nblk * tq)
        a, carry = _sb_weights(_dot_nt(q, k_ref[rows, :]), ntri, carry, mask)
        return _dot(a, v_ref[rows, :]), carry

    row = lax.broadcasted_iota(jnp.int32, (tq, tq), 0)
    col = lax.broadcasted_iota(jnp.int32, (tq, tq), 1)
    out, carry = block(qi, 1, jnp.zeros((tq, 1), F32), col < row)
    acc_ref[...] = out
    r_ref[...] = jnp.broadcast_to(carry, r_ref.shape)

    def step(j, nblk):
        out, carry = block(j, nblk, r_ref[:, 0:1], None)
        acc_ref[...] += out
        r_ref[...] = jnp.broadcast_to(carry, r_ref.shape)

    def body(i, c):
        step(qi - 4 - 4 * i, 4)
        return c

    lax.fori_loop(0, qi // 4, body, 0)

    @pl.when(qi % 4 >= 2)
    def _():
        step(qi % 2, 2)

    @pl.when(qi % 2 == 1)
    def _():
        step(0, 1)

    o_ref[...] = acc_ref[...].astype(o_ref.dtype)


def sb_prompt(q, k, v, ntri, tq=512):
    t, hd = q.shape
    h = hd // HEAD_DIM
    tq = min(tq, t)
    return pl.pallas_call(
        functools.partial(_sb_prompt_kernel, tq=tq),
        grid=(h, t // tq),
        in_specs=[
            pl.BlockSpec((tq, HEAD_DIM), lambda hi, qi: (qi, hi)),
            pl.BlockSpec((t, HEAD_DIM), lambda hi, qi: (0, hi)),
            pl.BlockSpec((t, HEAD_DIM), lambda hi, qi: (0, hi)),
            _resident(ntri.shape, lambda hi, qi: (0, 0)),
        ],
        out_specs=pl.BlockSpec((tq, HEAD_DIM), lambda hi, qi: (qi, hi)),
        out_shape=jax.ShapeDtypeStruct((t, hd), BF16),
        scratch_shapes=[pltpu.VMEM((tq, LANES), F32), pltpu.VMEM((tq, HEAD_DIM), F32)],
        compiler_params=_cparams(("parallel", "arbitrary")),
        name="sb_prompt",
    )(q, k, v, ntri)


def _sb_decode_kernel(q_ref, kn_ref, vn_ref, kc_ref, vc_ref, ntri_ref, o_ref, r_ref, acc_ref, *, heads, tk):
    kb = pl.program_id(1)
    tq = q_ref.shape[0]
    n_new = kn_ref.shape[0]

    def head_rows(h):
        return slice(h * tq, (h + 1) * tq)

    def stacked_scores(keys_of):
        return jnp.concatenate([_dot_nt(q_ref[:, _head_cols(h)], keys_of(h)) for h in range(heads)], axis=0)

    @pl.when(kb == 0)
    def _():
        z2 = stacked_scores(lambda h: kn_ref[:, _head_cols(h)])
        row = lax.rem(lax.broadcasted_iota(jnp.int32, (heads * tq, n_new), 0), tq)
        col = lax.broadcasted_iota(jnp.int32, (heads * tq, n_new), 1)
        a, carry = _sb_weights(z2, ntri_ref[0:n_new, 0:n_new], jnp.zeros((heads * tq, 1), F32), col < row)
        for h in range(heads):
            acc_ref[head_rows(h), :] = _dot(a[head_rows(h), :], vn_ref[:, _head_cols(h)])
        r_ref[...] = jnp.broadcast_to(carry, r_ref.shape)

    z2 = stacked_scores(lambda h: _head_rows(kc_ref, h, heads, tk).astype(BF16))
    a, carry = _sb_weights(z2, ntri_ref[...], r_ref[:, 0:1], None)
    for h in range(heads):
        acc_ref[head_rows(h), :] += _dot(a[head_rows(h), :], _head_rows(vc_ref, h, heads, tk).astype(BF16))
    r_ref[...] = jnp.broadcast_to(carry, r_ref.shape)

    @pl.when(kb == pl.num_programs(1) - 1)
    def _():
        for h in range(heads):
            o_ref[:, _head_cols(h)] = acc_ref[head_rows(h), :].astype(o_ref.dtype)


def sb_decode(q, k_new, v_new, k_cache, v_cache, ntri, heads, tk=512):
    b, tq, hd = q.shape
    n_new = k_new.shape[1]
    p_len = k_cache.shape[1] // heads
    tk = min(tk, p_len)
    nkb = p_len // tk
    return pl.pallas_call(
        functools.partial(_sb_decode_kernel, heads=heads, tk=tk),
        grid=(b, nkb),
        in_specs=[
            pl.BlockSpec((None, tq, hd), lambda bi, kb: (bi, 0, 0)),
            pl.BlockSpec((None, n_new, hd), lambda bi, kb: (bi, 0, 0)),
            pl.BlockSpec((None, n_new, hd), lambda bi, kb: (bi, 0, 0)),
            pl.BlockSpec((None, tk * heads, HEAD_DIM), lambda bi, kb: (bi, nkb - 1 - kb, 0)),
            pl.BlockSpec((None, tk * heads, HEAD_DIM), lambda bi, kb: (bi, nkb - 1 - kb, 0)),
            _resident(ntri.shape, lambda bi, kb: (0, 0)),
        ],
        out_specs=pl.BlockSpec((None, tq, hd), lambda bi, kb: (bi, 0, 0)),
        out_shape=jax.ShapeDtypeStruct((b, tq, hd), BF16),
        scratch_shapes=[pltpu.VMEM((heads * tq, LANES), F32), pltpu.VMEM((heads * tq, HEAD_DIM), F32)],
        compiler_params=_cparams(("parallel", "arbitrary")),
        name="sb_decode",
    )(q, k_new, v_new, k_cache, v_cache, ntri)


def _out_proj_kernel(xa_ref, xb_ref, w_ref, h_ref, o_ref):
    ka = xa_ref.shape[1]
    o_ref[...] = h_ref[...] + _dot(xa_ref[...], w_ref[:ka, :]) + _dot(xb_ref[...], w_ref[ka:, :])


def out_proj(xa, xb, col_b, w, h, tm=512):
    m, d = h.shape
    ka = w.shape[0] // 2
    tm = min(tm, m)
    return pl.pallas_call(
        _out_proj_kernel,
        grid=(m // tm,),
        in_specs=[
            pl.BlockSpec((tm, ka), lambda i: (i, 0)),
            pl.BlockSpec((tm, ka), lambda i: (i, col_b)),
            _resident(w.shape, lambda i: (0, 0)),
            pl.BlockSpec((tm, d), lambda i: (i, 0)),
        ],
        out_specs=pl.BlockSpec((tm, d), lambda i: (i, 0)),
        out_shape=jax.ShapeDtypeStruct((m, d), F32),
        compiler_params=_cparams(("parallel",)),
        name="out_proj",
    )(xa, xb, w, h)


def _ffn_kernel(h_ref, g_ref, wg_ref, wu_ref, wd_ref, gf_ref, o_ref, xn_ref, *, final_norm):
    fi = pl.program_id(1)

    @pl.when(fi == 0)
    def _():
        xn_ref[...] = _rms(h_ref[...], g_ref[...]).astype(BF16)
        o_ref[...] = jnp.zeros_like(o_ref)

    xn = xn_ref[...]
    gate = _dot(xn, wg_ref[...])
    up = _dot(xn, wu_ref[...])
    mid = (gate / (1.0 + jnp.exp(-gate)) * up).astype(BF16)
    width = wd_ref.shape[0]
    for c0 in range(0, o_ref.shape[1], width):
        o_ref[:, c0:c0 + width] += _dot(mid, wd_ref[:, c0:c0 + width])

    @pl.when(fi == pl.num_programs(1) - 1)
    def _():
        hn = h_ref[...] + o_ref[...]
        o_ref[...] = _rms(hn, gf_ref[...]) if final_norm else hn


def ffn(h, g, w_gate, w_up, w_down, g_final, final_norm, tm=1024, tf=512):
    m, d = h.shape
    f = w_gate.shape[1]
    tm = min(tm, m)
    return pl.pallas_call(
        functools.partial(_ffn_kernel, final_norm=final_norm),
        grid=(m // tm, f // tf),
        in_specs=[
            pl.BlockSpec((tm, d), lambda i, j: (i, 0), pipeline_mode=pl.Buffered(1)),
            _resident((1, d), lambda i, j: (0, 0)),
            pl.BlockSpec((d, tf), lambda i, j: (0, j)),
            pl.BlockSpec((d, tf), lambda i, j: (0, j)),
            pl.BlockSpec((tf, d), lambda i, j: (j, 0)),
            _resident((1, d), lambda i, j: (0, 0)),
        ],
        out_specs=pl.BlockSpec((tm, d), lambda i, j: (i, 0)),
        out_shape=jax.ShapeDtypeStruct((m, d), F32),
        scratch_shapes=[pltpu.VMEM((tm, d), BF16)],
        compiler_params=_cparams(("parallel", "arbitrary")),
        name="ffn",
    )(h, g, w_gate, w_up, w_down, g_final)


def _pad_rows(x, rows):
    return jnp.pad(x, ((0, 0), (0, rows - x.shape[1]), (0, 0)))


def _neg_tri_matrix():
    idx = np.arange(TRI)
    return jnp.asarray(-(idx[:, None] >= idx[None, :]).astype(np.float32), dtype=BF16)


def _layer0_groups(pool_w, fox_w):
    outs = [("f32",), ("qscaled",), ("f32", "bf16"), ("f32", "bf16")]
    groups, out_defs, col = [], [], 0
    for kinds in outs:
        width = pool_w if col == 0 else fox_w
        o = []
        for kind in kinds:
            o.append((len(out_defs), 0, kind))
            out_defs.append((width, F32 if kind == "f32" else BF16))
        groups.append((col, width, tuple(o)))
        col += width
    groups.append((col, LANES, ((len(out_defs), 0, "logsig"),)))
    out_defs.append((LANES, F32))
    return tuple(groups), out_defs


def _layer1_groups(d):
    half = d // 2
    groups = []
    for part, outs in enumerate([(0,), (1, 2), (3, 4)]):
        for c in range(2):
            kinds = ("qscaled",) if part == 0 else ("f32", "bf16")
            groups.append((part * d + c * half, half,
                           tuple((oi, c * half, kind) for oi, kind in zip(outs, kinds))))
    out_defs = [(d, BF16), (d, F32), (d, BF16), (d, F32), (d, BF16)]
    return tuple(groups), out_defs


def _run_trunk(x, n_past, pool_c, fk_c, fv_c, flf_c, sk_c, sv_c, wts):
    b, t, d = x.shape
    m = b * t
    h = x.reshape(m, d)
    pool_w = wts["w_pool"].shape[0] * wts["w_pool"].shape[1]
    fox_w = d - pool_w
    fox_h = fox_w // HEAD_DIM
    sb_h = d // HEAD_DIM
    first = fk_c is None
    zero_bias = jnp.zeros((1, LANES), F32)

    groups, out_defs = _layer0_groups(pool_w, fox_w)
    u, q, k, kb, v, vb, lf = norm_proj(h, wts["ln_mix0"], wts["w_in"], wts["b_forget"], groups, out_defs, tm=256)
    log_f = lf[:, :fox_h]
    if first:
        prefix = jnp.zeros((b, POOL_HALO, pool_w), F32)
    else:
        prefix = jnp.pad(pool_c, ((0, 0), (POOL_HALO - POOL_STATE, 0), (0, 0)))
    y_pool = pool_mixer(u.reshape(b, t, pool_w), prefix, wts["w_pool"], wts["pool_scale"], n_past)
    if t >= POOL_STATE:
        pool_state = u.reshape(b, t, pool_w)[:, t - POOL_STATE:]
    else:
        pool_state = jnp.concatenate([prefix[:, 1:], u.reshape(b, t, pool_w)], axis=1)[:, -POOL_STATE:]

    if first:
        tq = min(512, t)
        cum2 = cumsum_rows(lf)[:, :fox_h].T * LOG2E
        cq = jnp.broadcast_to(cum2[:, :, None], (fox_h, t, LANES))
        ck = cum2.reshape(fox_h, t // tq, 1, tq)
        o = fox_prompt(q, kb, vb, cq, ck, tq=tq)
    else:
        n_new = LANES
        lf_new = log_f.reshape(b, t, fox_h).transpose(1, 0, 2).reshape(t, b * fox_h)
        lf_all = jnp.concatenate([flf_c.transpose(1, 0, 2).reshape(n_past, b * fox_h), lf_new], axis=0)
        rows = n_past + t
        rows_pad = -(-rows // 512) * 512
        lf_all = jnp.pad(lf_all, ((0, rows_pad - rows), (0, LANES - b * fox_h)))
        cum2 = cumsum_rows(lf_all)[:rows, :b * fox_h].reshape(rows, b, fox_h).transpose(1, 2, 0) * LOG2E
        ck_cache = cum2[:, :, None, :n_past]
        cum_new = cum2[:, :, n_past:]
        ck_new = jnp.pad(cum_new, ((0, 0), (0, 0), (0, n_new - t)))[:, :, None, :]
        cq = jnp.broadcast_to(cum_new[:, :, :, None], (b, fox_h, t, LANES))
        o = fox_decode(q.reshape(b, t, fox_w), _pad_rows(kb.reshape(b, t, fox_w), n_new),
                       _pad_rows(vb.reshape(b, t, fox_w), n_new), fk_c, fv_c, cq, ck_cache, ck_new, fox_h)
        o = o.reshape(m, fox_w)
    h = out_proj(y_pool.reshape(m, pool_w), o, 0, wts["w_out_ab"], h)
    h = ffn(h, wts["ln_ffn0"], wts["w_gate0"], wts["w_up0"], wts["w_down0"], wts["ln_final"], False)

    groups1, out_defs1 = _layer1_groups(d)
    q1, k1, k1b, v1, v1b = norm_proj(h, wts["ln_mix1"], wts["w_qkv"], zero_bias, groups1, out_defs1, tm=256)
    ntri = _neg_tri_matrix()
    if first:
        o1 = sb_prompt(q1, k1b, v1b, ntri)
    else:
        n_new = LANES
        o1 = sb_decode(q1.reshape(b, t, d), _pad_rows(k1b.reshape(b, t, d), n_new),
                       _pad_rows(v1b.reshape(b, t, d), n_new), sk_c, sv_c, ntri, sb_h).reshape(m, d)
    h = out_proj(o1, o1, 1, wts["w_out_sb"], h)
    y = ffn(h, wts["ln_ffn1"], wts["w_gate1"], wts["w_up1"], wts["w_down1"], wts["ln_final"], True)

    return (y.reshape(b, t, d), pool_state[None], k.reshape(1, b, t, fox_h, HEAD_DIM),
            v.reshape(1, b, t, fox_h, HEAD_DIM), log_f.reshape(1, b, t, fox_h),
            k1.reshape(1, b, t, sb_h, HEAD_DIM), v1.reshape(1, b, t, sb_h, HEAD_DIM))


def kernel(x_prompt, x_sample, cache_pool, cache_fox_k, cache_fox_v, cache_fox_logf, cache_sb_k, cache_sb_v, ln_mix, w_in_ab, b_forget, w_pool, pool_scale, w_out_ab, w_qkv_sb, w_out_sb, ln_ffn, w_gate, w_up, w_down, ln_final):
    fox_h = b_forget.shape[-1]
    w_in = jnp.pad(w_in_ab[0], ((0, 0), (0, LANES - fox_h))).astype(BF16)
    wts = {
        "ln_mix0": ln_mix[0][None], "ln_mix1": ln_mix[1][None],
        "ln_ffn0": ln_ffn[0][None], "ln_ffn1": ln_ffn[1][None], "ln_final": ln_final[None],
        "w_in": w_in,
        "b_forget": jnp.pad(b_forget[0], (0, LANES - fox_h))[None],
        "w_pool": w_pool[0].astype(BF16), "pool_scale": pool_scale[0][None],
        "w_out_ab": w_out_ab[0].astype(BF16), "w_qkv": w_qkv_sb[0].astype(BF16),
        "w_out_sb": w_out_sb[0].astype(BF16),
        "w_gate0": w_gate[0].astype(BF16), "w_up0": w_up[0].astype(BF16), "w_down0": w_down[0].astype(BF16),
        "w_gate1": w_gate[1].astype(BF16), "w_up1": w_up[1].astype(BF16), "w_down1": w_down[1].astype(BF16),
    }
    out_p = _run_trunk(x_prompt, 0, None, None, None, None, None, None, wts)
    n_past = cache_fox_k.shape[2]
    bs = x_sample.shape[0]

    def rows_of(cache):
        return cache.reshape(bs, -1, HEAD_DIM)

    out_s = _run_trunk(x_sample, n_past, cache_pool[0], rows_of(cache_fox_k), rows_of(cache_fox_v),
                       cache_fox_logf[0], rows_of(cache_sb_k), rows_of(cache_sb_v), wts)
    return (out_p[0], out_s[0]) + out_p[1:] + out_s[1:]
```

```python
import functools

import numpy as np
import jax
import jax.numpy as jnp
from jax import lax
from jax.experimental import pallas as pl
from jax.experimental.pallas import tpu as pltpu

HEAD_DIM = 128
EPS = 1e-6
POOL_WINDOWS = (2, 4, 8, 16)
POOL_STATE = max(POOL_WINDOWS) - 1
POOL_HALO = 16
LANES = 128
TRI = 256
VMEM_LIMIT = 56 * 1024 * 1024
LOG2E = 1.4426950408889634
QK_SCALE2 = HEAD_DIM ** -0.5 * LOG2E

F32 = jnp.float32
BF16 = jnp.bfloat16


def _cparams(sem, vmem=VMEM_LIMIT):
    return pltpu.CompilerParams(dimension_semantics=sem, vmem_limit_bytes=vmem)


def _resident(shape, index_map):
    return pl.BlockSpec(shape, index_map, pipeline_mode=pl.Buffered(1))


def _dot(a, b):
    return jnp.dot(a, b, preferred_element_type=F32)


def _dot_nt(a, b):
    return lax.dot_general(a, b, (((1,), (1,)), ((), ())), preferred_element_type=F32)


def _rms(x, g):
    ms = jnp.mean(x * x, axis=-1, keepdims=True)
    return x * lax.rsqrt(ms + EPS) * g


def _softplus(z):
    return jnp.maximum(z, 0.0) + jnp.log1p(jnp.exp(-jnp.abs(z)))


def _softplus2(z2):
    return jnp.maximum(z2, 0.0) + jnp.log(1.0 + jnp.exp2(-jnp.abs(z2))) * LOG2E


def _tile_lanes(x, n):
    reps = n // LANES
    return x if reps == 1 else jnp.concatenate([x] * reps, axis=1)


def _head_cols(h):
    return slice(h * HEAD_DIM, (h + 1) * HEAD_DIM)


def _head_rows(ref, h, heads, n):
    return ref[pl.ds(h, n, stride=heads), :]


def _norm_proj_kernel(x_ref, g_ref, w_ref, b_ref, *out_refs, groups):
    xn = _rms(x_ref[...], g_ref[...]).astype(BF16)
    for w_col, width, outs in groups:
        acc = _dot(xn, w_ref[:, w_col:w_col + width])
        for out_idx, out_col, kind in outs:
            if kind == "logsig":
                val = -_softplus(-(acc + b_ref[...]))
            elif kind == "qscaled":
                val = acc * QK_SCALE2
            else:
                val = acc
            o = out_refs[out_idx]
            o[:, out_col:out_col + width] = val.astype(o.dtype)


def norm_proj(x, g, w, bias, groups, out_defs, tm):
    m, d = x.shape
    n = w.shape[1]
    tm = min(tm, m)
    out_shape = [jax.ShapeDtypeStruct((m, c), dt) for c, dt in out_defs]
    out_specs = [pl.BlockSpec((tm, c), lambda i: (i, 0)) for c, _ in out_defs]
    return pl.pallas_call(
        functools.partial(_norm_proj_kernel, groups=groups),
        grid=(m // tm,),
        in_specs=[
            pl.BlockSpec((tm, d), lambda i: (i, 0)),
            _resident((1, d), lambda i: (0, 0)),
            _resident((d, n), lambda i: (0, 0)),
            _resident((1, LANES), lambda i: (0, 0)),
        ],
        out_specs=out_specs,
        out_shape=out_shape,
        compiler_params=_cparams(("parallel",)),
        name="norm_proj",
    )(x, g, w, bias)


def _cumsum_kernel(x_ref, o_ref, carry_ref, *, tb):
    @pl.when(pl.program_id(0) == 0)
    def _():
        carry_ref[...] = jnp.zeros_like(carry_ref)

    x = x_ref[...]
    row = lax.broadcasted_iota(jnp.int32, (tb, tb), 0)
    col = lax.broadcasted_iota(jnp.int32, (tb, tb), 1)
    low = jnp.where(col <= row, 1.0, 0.0).astype(BF16)
    hi = x.astype(BF16)
    r1 = x - hi.astype(F32)
    mid = r1.astype(BF16)
    lo = (r1 - mid.astype(F32)).astype(BF16)
    out = _dot(low, hi) + _dot(low, mid) + _dot(low, lo) + carry_ref[0:1, :]
    o_ref[...] = out
    carry_ref[...] = jnp.broadcast_to(out[tb - 1:tb, :], carry_ref.shape)


def cumsum_rows(x, tb=512):
    t, c = x.shape
    tb = min(tb, t)
    return pl.pallas_call(
        functools.partial(_cumsum_kernel, tb=tb),
        grid=(t // tb,),
        in_specs=[pl.BlockSpec((tb, c), lambda i: (i, 0))],
        out_specs=pl.BlockSpec((tb, c), lambda i: (i, 0)),
        out_shape=jax.ShapeDtypeStruct((t, c), F32),
        scratch_shapes=[pltpu.VMEM((8, c), F32)],
        compiler_params=_cparams(("arbitrary",)),
        name="cumsum_rows",
    )(x)


def _pool_kernel(u_ref, pre_ref, w_ref, sc_ref, o_ref, buf_ref, *, tm, pos0, group):
    mi = pl.program_id(1)

    @pl.when(mi == 0)
    def _():
        buf_ref[0:POOL_HALO, :] = pre_ref[...]

    @pl.when(mi > 0)
    def _():
        buf_ref[0:POOL_HALO, :] = buf_ref[tm:tm + POOL_HALO, :]

    buf_ref[POOL_HALO:POOL_HALO + tm, :] = u_ref[...]
    pos = pos0 + mi * tm + lax.broadcasted_iota(jnp.int32, (tm, 1), 0)
    for gi, win in enumerate(POOL_WINDOWS):
        cs = slice(gi * group, (gi + 1) * group)
        tot = buf_ref[POOL_HALO:POOL_HALO + tm, cs]
        for back in range(1, win):
            tot = tot + buf_ref[POOL_HALO - back:POOL_HALO - back + tm, cs]
        count = jnp.minimum(pos + 1, win).astype(F32)
        diff = tot / count - u_ref[:, cs]
        y = _dot(diff.astype(BF16), w_ref[gi]) * sc_ref[:, cs]
        o_ref[:, cs] = y.astype(o_ref.dtype)


def pool_mixer(u, prefix, w_pool, scale, pos0, tm=512):
    b, t, c = u.shape
    tm = min(tm, t)
    group = c // len(POOL_WINDOWS)
    return pl.pallas_call(
        functools.partial(_pool_kernel, tm=tm, pos0=pos0, group=group),
        grid=(b, t // tm),
        in_specs=[
            pl.BlockSpec((None, tm, c), lambda bi, mi: (bi, mi, 0)),
            pl.BlockSpec((None, POOL_HALO, c), lambda bi, mi: (bi, 0, 0)),
            _resident(w_pool.shape, lambda bi, mi: (0, 0, 0)),
            _resident((1, c), lambda bi, mi: (0, 0)),
        ],
        out_specs=pl.BlockSpec((None, tm, c), lambda bi, mi: (bi, mi, 0)),
        out_shape=jax.ShapeDtypeStruct((b, t, c), BF16),
        scratch_shapes=[pltpu.VMEM((POOL_HALO + tm, c), F32)],
        compiler_params=_cparams(("arbitrary", "arbitrary")),
        name="pool_mixer",
    )(u, prefix, w_pool, scale)


def _fox_update(s2, v, m_prev, l_prev, acc_prev):
    m_new = jnp.maximum(m_prev, jnp.max(s2, axis=1, keepdims=True))
    alpha = jnp.exp2(m_prev - m_new)
    p = jnp.exp2(s2 - _tile_lanes(m_new, s2.shape[1]))
    l_new = alpha * l_prev + jnp.sum(p, axis=1, keepdims=True)
    acc_new = alpha * acc_prev + _dot(p.astype(BF16), v)
    return m_new, l_new, acc_new


FOX_DEAD = -160.0


def _fox_prompt_kernel(ck_end_ref, q_ref, k_ref, v_ref, cq_ref, ck_ref, o_ref, m_ref, l_ref, acc_ref, kmax_ref, *, tq):
    hi = pl.program_id(0)
    qi = pl.program_id(1)
    nq = pl.num_programs(1)

    @pl.when(qi == 0)
    def _():
        def norm_body(j, best):
            kb = k_ref[pl.ds(pl.multiple_of(j * tq, tq), tq), :].astype(F32)
            return jnp.maximum(best, jnp.sum(kb * kb, axis=1, keepdims=True))

        best = lax.fori_loop(0, nq, norm_body, jnp.zeros((tq, 1), F32))
        kmax_ref[...] = jnp.broadcast_to(jnp.sqrt(jnp.max(best, axis=0, keepdims=True)), kmax_ref.shape)

    m_ref[...] = jnp.full_like(m_ref, -jnp.inf)
    l_ref[...] = jnp.zeros_like(l_ref)
    acc_ref[...] = jnp.zeros_like(acc_ref)
    q = q_ref[...]
    qf = q.astype(F32)
    reach = jnp.sqrt(jnp.sum(qf * qf, axis=1, keepdims=True)) * kmax_ref[0:1, 0:1] + cq_ref[:, 0:1]

    def update(j, nblk, causal):
        width = nblk * tq
        rows = pl.ds(pl.multiple_of(j * tq, tq), width)
        ck = ck_ref[j] if nblk == 1 else jnp.concatenate([ck_ref[j + i] for i in range(nblk)], axis=1)
        s2 = _dot_nt(q, k_ref[rows, :]) + _tile_lanes(cq_ref[...], width) - ck
        if causal:
            row = lax.broadcasted_iota(jnp.int32, (tq, width), 0)
            col = lax.broadcasted_iota(jnp.int32, (tq, width), 1)
            s2 = jnp.where(col <= row, s2, -jnp.inf)
        m_ref[...], l_ref[...], acc_ref[...] = _fox_update(s2, v_ref[rows, :], m_ref[...], l_ref[...], acc_ref[...])

    def live(j_last):
        return jnp.max(reach - m_ref[:, 0:1]) - ck_end_ref[hi, j_last] > FOX_DEAD

    update(qi, 1, True)

    @pl.when(qi >= 1)
    def _():
        update(qi - 1, 1, False)

    n_rest = jnp.maximum(qi - 1, 0)

    def cond(state):
        i, alive = state
        return jnp.logical_and(i < n_rest // 4, alive)

    def body(state):
        i, _ = state
        update(n_rest - 4 - 4 * i, 4, False)
        return i + 1, live(jnp.maximum(n_rest - 5 - 4 * i, 0))

    lax.while_loop(cond, body, (0, live(jnp.maximum(n_rest - 1, 0))))
    rem = n_rest % 4

    @pl.when(jnp.logical_and(rem >= 2, live(jnp.maximum(rem - 1, 0))))
    def _():
        update(rem % 2, 2, False)

    @pl.when(jnp.logical_and(rem % 2 == 1, live(0)))
    def _():
        update(0, 1, False)

    o_ref[...] = (acc_ref[...] / l_ref[...]).astype(o_ref.dtype)


def fox_prompt(q, k, v, cq, ck, ck_end, tq=512):
    t, hd = q.shape
    h = hd // HEAD_DIM
    tq = min(tq, t)
    nq = t // tq
    return pl.pallas_call(
        functools.partial(_fox_prompt_kernel, tq=tq),
        grid=(h, nq),
        in_specs=[
            pl.BlockSpec(memory_space=pltpu.SMEM),
            pl.BlockSpec((tq, HEAD_DIM), lambda hi, qi: (qi, hi)),
            pl.BlockSpec((t, HEAD_DIM), lambda hi, qi: (0, hi)),
            pl.BlockSpec((t, HEAD_DIM), lambda hi, qi: (0, hi)),
            pl.BlockSpec((None, tq, LANES), lambda hi, qi: (hi, qi, 0)),
            pl.BlockSpec((None, nq, 1, tq), lambda hi, qi: (hi, 0, 0, 0)),
        ],
        out_specs=pl.BlockSpec((tq, HEAD_DIM), lambda hi, qi: (qi, hi)),
        out_shape=jax.ShapeDtypeStruct((t, hd), BF16),
        scratch_shapes=[pltpu.VMEM((tq, LANES), F32), pltpu.VMEM((tq, LANES), F32),
                        pltpu.VMEM((tq, HEAD_DIM), F32), pltpu.VMEM((8, LANES), F32)],
        compiler_params=_cparams(("arbitrary", "arbitrary")),
        name="fox_prompt",
    )(ck_end, q, k, v, cq, ck)


def _fox_decode_kernel(q_ref, kn_ref, vn_ref, kc_ref, vc_ref, cq_ref, ckc_ref, ckn_ref, o_ref,
                       m_ref, l_ref, acc_ref, *, heads, tk):
    kb = pl.program_id(1)
    tq = q_ref.shape[0]
    n_new = kn_ref.shape[0]

    @pl.when(kb == 0)
    def _():
        row = lax.broadcasted_iota(jnp.int32, (tq, n_new), 0)
        col = lax.broadcasted_iota(jnp.int32, (tq, n_new), 1)
        for h in range(heads):
            hc = _head_cols(h)
            s2 = _dot_nt(q_ref[:, hc], kn_ref[:, hc]) + _tile_lanes(cq_ref[h], n_new) - ckn_ref[h]
            s2 = jnp.where(col <= row, s2, -jnp.inf)
            m = jnp.max(s2, axis=1, keepdims=True)
            p = jnp.exp2(s2 - m)
            m_ref[h] = jnp.broadcast_to(m, (tq, LANES))
            l_ref[h] = jnp.broadcast_to(jnp.sum(p, axis=1, keepdims=True), (tq, LANES))
            acc_ref[h] = _dot(p.astype(BF16), vn_ref[:, hc])

    for h in range(heads):
        k_h = _head_rows(kc_ref, h, heads, tk).astype(BF16)
        v_h = _head_rows(vc_ref, h, heads, tk).astype(BF16)
        s2 = _dot_nt(q_ref[:, _head_cols(h)], k_h) + _tile_lanes(cq_ref[h], tk) - ckc_ref[h]
        m_ref[h], l_ref[h], acc_ref[h] = _fox_update(s2, v_h, m_ref[h], l_ref[h], acc_ref[h])

    @pl.when(kb == pl.num_programs(1) - 1)
    def _():
        for h in range(heads):
            o_ref[:, _head_cols(h)] = (acc_ref[h] / l_ref[h]).astype(o_ref.dtype)


def fox_decode(q, k_new, v_new, k_cache, v_cache, cq, ck_cache, ck_new, heads, tk=1024):
    b, tq, hd = q.shape
    n_new = k_new.shape[1]
    p_len = k_cache.shape[1] // heads
    tk = min(tk, p_len)
    return pl.pallas_call(
        functools.partial(_fox_decode_kernel, heads=heads, tk=tk),
        grid=(b, p_len // tk),
        in_specs=[
            pl.BlockSpec((None, tq, hd), lambda bi, kb: (bi, 0, 0)),
            pl.BlockSpec((None, n_new, hd), lambda bi, kb: (bi, 0, 0)),
            pl.BlockSpec((None, n_new, hd), lambda bi, kb: (bi, 0, 0)),
            pl.BlockSpec((None, tk * heads, HEAD_DIM), lambda bi, kb: (bi, kb, 0)),
            pl.BlockSpec((None, tk * heads, HEAD_DIM), lambda bi, kb: (bi, kb, 0)),
            pl.BlockSpec((None, heads, tq, LANES), lambda bi, kb: (bi, 0, 0, 0)),
            pl.BlockSpec((None, heads, 1, tk), lambda bi, kb: (bi, 0, 0, kb)),
            pl.BlockSpec((None, heads, 1, n_new), lambda bi, kb: (bi, 0, 0, 0)),
        ],
        out_specs=pl.BlockSpec((None, tq, hd), lambda bi, kb: (bi, 0, 0)),
        out_shape=jax.ShapeDtypeStruct((b, tq, hd), BF16),
        scratch_shapes=[pltpu.VMEM((heads, tq, LANES), F32), pltpu.VMEM((heads, tq, LANES), F32),
                        pltpu.VMEM((heads, tq, HEAD_DIM), F32)],
        compiler_params=_cparams(("parallel", "arbitrary")),
        name="fox_decode",
    )(q, k_new, v_new, k_cache, v_cache, cq, ck_cache, ck_new)


SB_DEAD = -160.0


def _suffix_sums(sp, ntri, carry):
    width = ntri.shape[0]
    n = sp.shape[1]
    parts = []
    for c0 in range(n - width, -1, -width):
        piece = sp[:, c0:c0 + width]
        hi = piece.astype(BF16)
        lo = (piece - hi.astype(F32)).astype(BF16)
        c = _dot(hi, ntri) + _dot(lo, ntri) + carry
        carry = c[:, 0:1]
        parts.append(c)
    parts.reverse()
    return (parts[0] if len(parts) == 1 else jnp.concatenate(parts, axis=1)), carry


def _sb_weights(z2, ntri, carry, mask):
    sp = _softplus2(z2)
    if mask is not None:
        sp = jnp.where(mask, sp, 0.0)
    c, new_carry = _suffix_sums(sp, ntri, carry)
    a = jnp.exp2(z2 + c)
    if mask is not None:
        a = jnp.where(mask, a, 0.0)
    return a.astype(BF16), new_carry


def _sb_prompt_kernel(q_ref, k_ref, v_ref, ntri_ref, o_ref, r_ref, acc_ref, *, tq):
    qi = pl.program_id(1)
    q = q_ref[...]
    ntri = ntri_ref[...]

    def block(j, nblk, carry, mask):
        rows = pl.ds(pl.multiple_of(j * tq, tq), nblk * tq)
        a, carry = _sb_weights(_dot_nt(q, k_ref[rows, :]), ntri, carry, mask)
        return _dot(a, v_ref[rows, :]), carry

    row = lax.broadcasted_iota(jnp.int32, (tq, tq), 0)
    col = lax.broadcasted_iota(jnp.int32, (tq, tq), 1)
    out, carry = block(qi, 1, jnp.zeros((tq, 1), F32), col < row)
    acc_ref[...] = out
    r_ref[...] = jnp.broadcast_to(carry, r_ref.shape)

    def step(j, nblk):
        out, carry = block(j, nblk, r_ref[:, 0:1], None)
        acc_ref[...] += out
        r_ref[...] = jnp.broadcast_to(carry, r_ref.shape)

    def live():
        return jnp.max(r_ref[...]) > SB_DEAD

    @pl.when(qi >= 1)
    def _():
        step(qi - 1, 1)

    n_rest = jnp.maximum(qi - 1, 0)

    def cond(state):
        i, alive = state
        return jnp.logical_and(i < n_rest // 4, alive)

    def body(state):
        i, _ = state
        step(n_rest - 4 - 4 * i, 4)
        return i + 1, live()

    lax.while_loop(cond, body, (0, live()))

    @pl.when(jnp.logical_and(n_rest % 4 >= 2, live()))
    def _():
        step(n_rest % 2, 2)

    @pl.when(jnp.logical_and(n_rest % 2 == 1, live()))
    def _():
        step(0, 1)

    o_ref[...] = acc_ref[...].astype(o_ref.dtype)


def sb_prompt(q, k, v, ntri, tq=512):
    t, hd = q.shape
    h = hd // HEAD_DIM
    tq = min(tq, t)
    return pl.pallas_call(
        functools.partial(_sb_prompt_kernel, tq=tq),
        grid=(h, t // tq),
        in_specs=[
            pl.BlockSpec((tq, HEAD_DIM), lambda hi, qi: (qi, hi)),
            pl.BlockSpec((t, HEAD_DIM), lambda hi, qi: (0, hi)),
            pl.BlockSpec((t, HEAD_DIM), lambda hi, qi: (0, hi)),
            _resident(ntri.shape, lambda hi, qi: (0, 0)),
        ],
        out_specs=pl.BlockSpec((tq, HEAD_DIM), lambda hi, qi: (qi, hi)),
        out_shape=jax.ShapeDtypeStruct((t, hd), BF16),
        scratch_shapes=[pltpu.VMEM((tq, LANES), F32), pltpu.VMEM((tq, HEAD_DIM), F32)],
        compiler_params=_cparams(("parallel", "arbitrary")),
        name="sb_prompt",
    )(q, k, v, ntri)


def _sb_decode_kernel(q_ref, kn_ref, vn_ref, kc_ref, vc_ref, ntri_ref, o_ref, r_ref, acc_ref, *, heads, tk):
    kb = pl.program_id(1)
    tq = q_ref.shape[0]
    n_new = kn_ref.shape[0]

    def head_rows(h):
        return slice(h * tq, (h + 1) * tq)

    def stacked_scores(keys_of):
        return jnp.concatenate([_dot_nt(q_ref[:, _head_cols(h)], keys_of(h)) for h in range(heads)], axis=0)

    @pl.when(kb == 0)
    def _():
        z2 = stacked_scores(lambda h: kn_ref[:, _head_cols(h)])
        row = lax.rem(lax.broadcasted_iota(jnp.int32, (heads * tq, n_new), 0), tq)
        col = lax.broadcasted_iota(jnp.int32, (heads * tq, n_new), 1)
        a, carry = _sb_weights(z2, ntri_ref[0:n_new, 0:n_new], jnp.zeros((heads * tq, 1), F32), col < row)
        for h in range(heads):
            acc_ref[head_rows(h), :] = _dot(a[head_rows(h), :], vn_ref[:, _head_cols(h)])
        r_ref[...] = jnp.broadcast_to(carry, r_ref.shape)

    @pl.when(jnp.max(r_ref[...]) > SB_DEAD)
    def _():
        z2 = stacked_scores(lambda h: _head_rows(kc_ref, h, heads, tk).astype(BF16))
        a, carry = _sb_weights(z2, ntri_ref[...], r_ref[:, 0:1], None)
        for h in range(heads):
            acc_ref[head_rows(h), :] += _dot(a[head_rows(h), :], _head_rows(vc_ref, h, heads, tk).astype(BF16))
        r_ref[...] = jnp.broadcast_to(carry, r_ref.shape)

    @pl.when(kb == pl.num_programs(1) - 1)
    def _():
        for h in range(heads):
            o_ref[:, _head_cols(h)] = acc_ref[head_rows(h), :].astype(o_ref.dtype)


def sb_decode(q, k_new, v_new, k_cache, v_cache, ntri, heads, tk=512):
    b, tq, hd = q.shape
    n_new = k_new.shape[1]
    p_len = k_cache.shape[1] // heads
    tk = min(tk, p_len)
    nkb = p_len // tk
    return pl.pallas_call(
        functools.partial(_sb_decode_kernel, heads=heads, tk=tk),
        grid=(b, nkb),
        in_specs=[
            pl.BlockSpec((None, tq, hd), lambda bi, kb: (bi, 0, 0)),
            pl.BlockSpec((None, n_new, hd), lambda bi, kb: (bi, 0, 0)),
            pl.BlockSpec((None, n_new, hd), lambda bi, kb: (bi, 0, 0)),
            pl.BlockSpec((None, tk * heads, HEAD_DIM), lambda bi, kb: (bi, nkb - 1 - kb, 0)),
            pl.BlockSpec((None, tk * heads, HEAD_DIM), lambda bi, kb: (bi, nkb - 1 - kb, 0)),
            _resident(ntri.shape, lambda bi, kb: (0, 0)),
        ],
        out_specs=pl.BlockSpec((None, tq, hd), lambda bi, kb: (bi, 0, 0)),
        out_shape=jax.ShapeDtypeStruct((b, tq, hd), BF16),
        scratch_shapes=[pltpu.VMEM((heads * tq, LANES), F32), pltpu.VMEM((heads * tq, HEAD_DIM), F32)],
        compiler_params=_cparams(("parallel", "arbitrary")),
        name="sb_decode",
    )(q, k_new, v_new, k_cache, v_cache, ntri)


def _out_proj_kernel(xa_ref, xb_ref, w_ref, h_ref, o_ref):
    ka = xa_ref.shape[1]
    o_ref[...] = h_ref[...] + _dot(xa_ref[...], w_ref[:ka, :]) + _dot(xb_ref[...], w_ref[ka:, :])


def out_proj(xa, xb, col_b, w, h, tm=512):
    m, d = h.shape
    ka = w.shape[0] // 2
    tm = min(tm, m)
    return pl.pallas_call(
        _out_proj_kernel,
        grid=(m // tm,),
        in_specs=[
            pl.BlockSpec((tm, ka), lambda i: (i, 0)),
            pl.BlockSpec((tm, ka), lambda i: (i, col_b)),
            _resident(w.shape, lambda i: (0, 0)),
            pl.BlockSpec((tm, d), lambda i: (i, 0)),
        ],
        out_specs=pl.BlockSpec((tm, d), lambda i: (i, 0)),
        out_shape=jax.ShapeDtypeStruct((m, d), F32),
        compiler_params=_cparams(("parallel",)),
        name="out_proj",
    )(xa, xb, w, h)


def _ffn_kernel(h_ref, g_ref, wg_ref, wu_ref, wd_ref, gf_ref, o_ref, xn_ref, *, final_norm):
    fi = pl.program_id(1)

    @pl.when(fi == 0)
    def _():
        xn_ref[...] = _rms(h_ref[...], g_ref[...]).astype(BF16)
        o_ref[...] = jnp.zeros_like(o_ref)

    xn = xn_ref[...]
    gate = _dot(xn, wg_ref[...])
    up = _dot(xn, wu_ref[...])
    mid = (gate / (1.0 + jnp.exp(-gate)) * up).astype(BF16)
    width = wd_ref.shape[0]
    for c0 in range(0, o_ref.shape[1], width):
        o_ref[:, c0:c0 + width] += _dot(mid, wd_ref[:, c0:c0 + width])

    @pl.when(fi == pl.num_programs(1) - 1)
    def _():
        hn = h_ref[...] + o_ref[...]
        o_ref[...] = _rms(hn, gf_ref[...]) if final_norm else hn


def ffn(h, g, w_gate, w_up, w_down, g_final, final_norm, tm=1024, tf=512):
    m, d = h.shape
    f = w_gate.shape[1]
    tm = min(tm, m)
    return pl.pallas_call(
        functools.partial(_ffn_kernel, final_norm=final_norm),
        grid=(m // tm, f // tf),
        in_specs=[
            pl.BlockSpec((tm, d), lambda i, j: (i, 0), pipeline_mode=pl.Buffered(1)),
            _resident((1, d), lambda i, j: (0, 0)),
            pl.BlockSpec((d, tf), lambda i, j: (0, j)),
            pl.BlockSpec((d, tf), lambda i, j: (0, j)),
            pl.BlockSpec((tf, d), lambda i, j: (j, 0)),
            _resident((1, d), lambda i, j: (0, 0)),
        ],
        out_specs=pl.BlockSpec((tm, d), lambda i, j: (i, 0)),
        out_shape=jax.ShapeDtypeStruct((m, d), F32),
        scratch_shapes=[pltpu.VMEM((tm, d), BF16)],
        compiler_params=_cparams(("parallel", "arbitrary")),
        name="ffn",
    )(h, g, w_gate, w_up, w_down, g_final)


def _pad_rows(x, rows):
    return jnp.pad(x, ((0, 0), (0, rows - x.shape[1]), (0, 0)))


def _neg_tri_matrix():
    idx = np.arange(TRI)
    return jnp.asarray(-(idx[:, None] >= idx[None, :]).astype(np.float32), dtype=BF16)


def _layer0_groups(pool_w, fox_w):
    outs = [("f32",), ("qscaled",), ("f32", "bf16"), ("f32", "bf16")]
    groups, out_defs, col = [], [], 0
    for kinds in outs:
        width = pool_w if col == 0 else fox_w
        o = []
        for kind in kinds:
            o.append((len(out_defs), 0, kind))
            out_defs.append((width, F32 if kind == "f32" else BF16))
        groups.append((col, width, tuple(o)))
        col += width
    groups.append((col, LANES, ((len(out_defs), 0, "logsig"),)))
    out_defs.append((LANES, F32))
    return tuple(groups), out_defs


def _layer1_groups(d):
    half = d // 2
    groups = []
    for part, outs in enumerate([(0,), (1, 2), (3, 4)]):
        for c in range(2):
            kinds = ("qscaled",) if part == 0 else ("f32", "bf16")
            groups.append((part * d + c * half, half,
                           tuple((oi, c * half, kind) for oi, kind in zip(outs, kinds))))
    out_defs = [(d, BF16), (d, F32), (d, BF16), (d, F32), (d, BF16)]
    return tuple(groups), out_defs


def _run_trunk(x, n_past, pool_c, fk_c, fv_c, flf_c, sk_c, sv_c, wts):
    b, t, d = x.shape
    m = b * t
    h = x.reshape(m, d)
    pool_w = wts["w_pool"].shape[0] * wts["w_pool"].shape[1]
    fox_w = d - pool_w
    fox_h = fox_w // HEAD_DIM
    sb_h = d // HEAD_DIM
    first = fk_c is None
    zero_bias = jnp.zeros((1, LANES), F32)

    groups, out_defs = _layer0_groups(pool_w, fox_w)
    u, q, k, kb, v, vb, lf = norm_proj(h, wts["ln_mix0"], wts["w_in"], wts["b_forget"], groups, out_defs, tm=256)
    log_f = lf[:, :fox_h]
    if first:
        prefix = jnp.zeros((b, POOL_HALO, pool_w), F32)
    else:
        prefix = jnp.pad(pool_c, ((0, 0), (POOL_HALO - POOL_STATE, 0), (0, 0)))
    y_pool = pool_mixer(u.reshape(b, t, pool_w), prefix, wts["w_pool"], wts["pool_scale"], n_past)
    if t >= POOL_STATE:
        pool_state = u.reshape(b, t, pool_w)[:, t - POOL_STATE:]
    else:
        pool_state = jnp.concatenate([prefix[:, 1:], u.reshape(b, t, pool_w)], axis=1)[:, -POOL_STATE:]

    if first:
        tq = min(512, t)
        cum2 = cumsum_rows(lf)[:, :fox_h].T * LOG2E
        cq = jnp.broadcast_to(cum2[:, :, None], (fox_h, t, LANES))
        ck = cum2.reshape(fox_h, t // tq, 1, tq)
        o = fox_prompt(q, kb, vb, cq, ck, ck[:, :, 0, tq - 1], tq=tq)
    else:
        n_new = LANES
        lf_new = log_f.reshape(b, t, fox_h).transpose(1, 0, 2).reshape(t, b * fox_h)
        lf_all = jnp.concatenate([flf_c.transpose(1, 0, 2).reshape(n_past, b * fox_h), lf_new], axis=0)
        rows = n_past + t
        rows_pad = -(-rows // 512) * 512
        lf_all = jnp.pad(lf_all, ((0, rows_pad - rows), (0, LANES - b * fox_h)))
        cum2 = cumsum_rows(lf_all)[:rows, :b * fox_h].reshape(rows, b, fox_h).transpose(1, 2, 0) * LOG2E
        ck_cache = cum2[:, :, None, :n_past]
        cum_new = cum2[:, :, n_past:]
        ck_new = jnp.pad(cum_new, ((0, 0), (0, 0), (0, n_new - t)))[:, :, None, :]
        cq = jnp.broadcast_to(cum_new[:, :, :, None], (b, fox_h, t, LANES))
        o = fox_decode(q.reshape(b, t, fox_w), _pad_rows(kb.reshape(b, t, fox_w), n_new),
                       _pad_rows(vb.reshape(b, t, fox_w), n_new), fk_c, fv_c, cq, ck_cache, ck_new, fox_h)
        o = o.reshape(m, fox_w)
    h = out_proj(y_pool.reshape(m, pool_w), o, 0, wts["w_out_ab"], h)
    h = ffn(h, wts["ln_ffn0"], wts["w_gate0"], wts["w_up0"], wts["w_down0"], wts["ln_final"], False)

    groups1, out_defs1 = _layer1_groups(d)
    q1, k1, k1b, v1, v1b = norm_proj(h, wts["ln_mix1"], wts["w_qkv"], zero_bias, groups1, out_defs1, tm=256)
    ntri = _neg_tri_matrix()
    if first:
        o1 = sb_prompt(q1, k1b, v1b, ntri)
    else:
        n_new = LANES
        o1 = sb_decode(q1.reshape(b, t, d), _pad_rows(k1b.reshape(b, t, d), n_new),
                       _pad_rows(v1b.reshape(b, t, d), n_new), sk_c, sv_c, ntri, sb_h).reshape(m, d)
    h = out_proj(o1, o1, 1, wts["w_out_sb"], h)
    y = ffn(h, wts["ln_ffn1"], wts["w_gate1"], wts["w_up1"], wts["w_down1"], wts["ln_final"], True)

    return (y.reshape(b, t, d), pool_state[None], k.reshape(1, b, t, fox_h, HEAD_DIM),
            v.reshape(1, b, t, fox_h, HEAD_DIM), log_f.reshape(1, b, t, fox_h),
            k1.reshape(1, b, t, sb_h, HEAD_DIM), v1.reshape(1, b, t, sb_h, HEAD_DIM))


def kernel(x_prompt, x_sample, cache_pool, cache_fox_k, cache_fox_v, cache_fox_logf, cache_sb_k, cache_sb_v, ln_mix, w_in_ab, b_forget, w_pool, pool_scale, w_out_ab, w_qkv_sb, w_out_sb, ln_ffn, w_gate, w_up, w_down, ln_final):
    fox_h = b_forget.shape[-1]
    w_in = jnp.pad(w_in_ab[0], ((0, 0), (0, LANES - fox_h))).astype(BF16)
    wts = {
        "ln_mix0": ln_mix[0][None], "ln_mix1": ln_mix[1][None],
        "ln_ffn0": ln_ffn[0][None], "ln_ffn1": ln_ffn[1][None], "ln_final": ln_final[None],
        "w_in": w_in,
        "b_forget": jnp.pad(b_forget[0], (0, LANES - fox_h))[None],
        "w_pool": w_pool[0].astype(BF16), "pool_scale": pool_scale[0][None],
        "w_out_ab": w_out_ab[0].astype(BF16), "w_qkv": w_qkv_sb[0].astype(BF16),
        "w_out_sb": w_out_sb[0].astype(BF16),
        "w_gate0": w_gate[0].astype(BF16), "w_up0": w_up[0].astype(BF16), "w_down0": w_down[0].astype(BF16),
        "w_gate1": w_gate[1].astype(BF16), "w_up1": w_up[1].astype(BF16), "w_down1": w_down[1].astype(BF16),
    }
    out_p = _run_trunk(x_prompt, 0, None, None, None, None, None, None, wts)
    n_past = cache_fox_k.shape[2]
    bs = x_sample.shape[0]

    def rows_of(cache):
        return cache.reshape(bs, -1, HEAD_DIM)

    out_s = _run_trunk(x_sample, n_past, cache_pool[0], rows_of(cache_fox_k), rows_of(cache_fox_v),
                       cache_fox_logf[0], rows_of(cache_sb_k), rows_of(cache_sb_v), wts)
    return (out_p[0], out_s[0]) + out_p[1:] + out_s[1:]
```

```python
import functools

import numpy as np
import jax
import jax.numpy as jnp
from jax import lax
from jax.experimental import pallas as pl
from jax.experimental.pallas import tpu as pltpu

HEAD_DIM = 128
EPS = 1e-6
POOL_WINDOWS = (2, 4, 8, 16)
POOL_STATE = max(POOL_WINDOWS) - 1
POOL_HALO = 16
LANES = 128
TRI = 256
VMEM_LIMIT = 56 * 1024 * 1024
LOG2E = 1.4426950408889634
QK_SCALE2 = HEAD_DIM ** -0.5 * LOG2E

F32 = jnp.float32
BF16 = jnp.bfloat16


def _cparams(sem, vmem=VMEM_LIMIT):
    return pltpu.CompilerParams(dimension_semantics=sem, vmem_limit_bytes=vmem)


def _resident(shape, index_map):
    return pl.BlockSpec(shape, index_map, pipeline_mode=pl.Buffered(1))


def _dot(a, b):
    return jnp.dot(a, b, preferred_element_type=F32)


def _dot_nt(a, b):
    return lax.dot_general(a, b, (((1,), (1,)), ((), ())), preferred_element_type=F32)


def _rms(x, g):
    ms = jnp.mean(x * x, axis=-1, keepdims=True)
    return x * lax.rsqrt(ms + EPS) * g


def _softplus(z):
    return jnp.maximum(z, 0.0) + jnp.log1p(jnp.exp(-jnp.abs(z)))


def _softplus2(z2):
    return jnp.maximum(z2, 0.0) + jnp.log(1.0 + jnp.exp2(-jnp.abs(z2))) * LOG2E


def _tile_lanes(x, n):
    reps = n // LANES
    return x if reps == 1 else jnp.concatenate([x] * reps, axis=1)


def _head_cols(h):
    return slice(h * HEAD_DIM, (h + 1) * HEAD_DIM)


def _head_rows(ref, h, heads, n):
    return ref[pl.ds(h, n, stride=heads), :]


def _norm_proj_kernel(x_ref, g_ref, w_ref, w2_ref, b_ref, *out_refs, groups):
    xn = _rms(x_ref[...], g_ref[...]).astype(BF16)
    for w_col, width, outs in groups:
        acc = _dot(xn, w2_ref[...] if w_col is None else w_ref[:, w_col:w_col + width])
        for out_idx, out_col, kind in outs:
            if kind == "logsig":
                val = -_softplus(-(acc + b_ref[...]))
            elif kind == "qscaled":
                val = acc * QK_SCALE2
            else:
                val = acc
            o = out_refs[out_idx]
            o[:, out_col:out_col + width] = val.astype(o.dtype)


def norm_proj(x, g, w, w2, bias, groups, out_defs, tm):
    m, d = x.shape
    n = w.shape[1]
    tm = min(tm, m)
    out_shape = [jax.ShapeDtypeStruct((m, c), dt) for c, dt in out_defs]
    out_specs = [pl.BlockSpec((tm, c), lambda i: (i, 0)) for c, _ in out_defs]
    return pl.pallas_call(
        functools.partial(_norm_proj_kernel, groups=groups),
        grid=(m // tm,),
        in_specs=[
            pl.BlockSpec((tm, d), lambda i: (i, 0)),
            _resident((1, d), lambda i: (0, 0)),
            _resident((d, n), lambda i: (0, 0)),
            _resident((d, LANES), lambda i: (0, 0)),
            _resident((1, LANES), lambda i: (0, 0)),
        ],
        out_specs=out_specs,
        out_shape=out_shape,
        compiler_params=_cparams(("parallel",)),
        name="norm_proj",
    )(x, g, w, w2, bias)


def _cumsum_kernel(x_ref, o_ref, carry_ref, *, tb):
    @pl.when(pl.program_id(0) == 0)
    def _():
        carry_ref[...] = jnp.zeros_like(carry_ref)

    x = x_ref[...]
    row = lax.broadcasted_iota(jnp.int32, (tb, tb), 0)
    col = lax.broadcasted_iota(jnp.int32, (tb, tb), 1)
    low = jnp.where(col <= row, 1.0, 0.0).astype(BF16)
    hi = x.astype(BF16)
    r1 = x - hi.astype(F32)
    mid = r1.astype(BF16)
    lo = (r1 - mid.astype(F32)).astype(BF16)
    out = _dot(low, hi) + _dot(low, mid) + _dot(low, lo) + carry_ref[0:1, :]
    o_ref[...] = out
    carry_ref[...] = jnp.broadcast_to(out[tb - 1:tb, :], carry_ref.shape)


def cumsum_rows(x, tb=512):
    t, c = x.shape
    tb = min(tb, t)
    return pl.pallas_call(
        functools.partial(_cumsum_kernel, tb=tb),
        grid=(t // tb,),
        in_specs=[pl.BlockSpec((tb, c), lambda i: (i, 0))],
        out_specs=pl.BlockSpec((tb, c), lambda i: (i, 0)),
        out_shape=jax.ShapeDtypeStruct((t, c), F32),
        scratch_shapes=[pltpu.VMEM((8, c), F32)],
        compiler_params=_cparams(("arbitrary",)),
        name="cumsum_rows",
    )(x)


def _pool_kernel(u_ref, pre_ref, w_ref, sc_ref, o_ref, buf_ref, *, tm, pos0, group):
    mi = pl.program_id(1)

    @pl.when(mi == 0)
    def _():
        buf_ref[0:POOL_HALO, :] = pre_ref[...]

    @pl.when(mi > 0)
    def _():
        buf_ref[0:POOL_HALO, :] = buf_ref[tm:tm + POOL_HALO, :]

    buf_ref[POOL_HALO:POOL_HALO + tm, :] = u_ref[...]
    pos = pos0 + mi * tm + lax.broadcasted_iota(jnp.int32, (tm, 1), 0)
    for gi, win in enumerate(POOL_WINDOWS):
        cs = slice(gi * group, (gi + 1) * group)
        tot = buf_ref[POOL_HALO:POOL_HALO + tm, cs]
        for back in range(1, win):
            tot = tot + buf_ref[POOL_HALO - back:POOL_HALO - back + tm, cs]
        count = jnp.minimum(pos + 1, win).astype(F32)
        diff = tot / count - u_ref[:, cs]
        y = _dot(diff.astype(BF16), w_ref[gi]) * sc_ref[:, cs]
        o_ref[:, cs] = y.astype(o_ref.dtype)


def pool_mixer(u, prefix, w_pool, scale, pos0, tm=512):
    b, t, c = u.shape
    tm = min(tm, t)
    group = c // len(POOL_WINDOWS)
    return pl.pallas_call(
        functools.partial(_pool_kernel, tm=tm, pos0=pos0, group=group),
        grid=(b, t // tm),
        in_specs=[
            pl.BlockSpec((None, tm, c), lambda bi, mi: (bi, mi, 0)),
            pl.BlockSpec((None, POOL_HALO, c), lambda bi, mi: (bi, 0, 0)),
            _resident(w_pool.shape, lambda bi, mi: (0, 0, 0)),
            _resident((1, c), lambda bi, mi: (0, 0)),
        ],
        out_specs=pl.BlockSpec((None, tm, c), lambda bi, mi: (bi, mi, 0)),
        out_shape=jax.ShapeDtypeStruct((b, t, c), BF16),
        scratch_shapes=[pltpu.VMEM((POOL_HALO + tm, c), F32)],
        compiler_params=_cparams(("arbitrary", "arbitrary")),
        name="pool_mixer",
    )(u, prefix, w_pool, scale)


def _fox_update(s2, v, m_prev, l_prev, acc_prev):
    m_new = jnp.maximum(m_prev, jnp.max(s2, axis=1, keepdims=True))
    alpha = jnp.exp2(m_prev - m_new)
    p = jnp.exp2(s2 - _tile_lanes(m_new, s2.shape[1]))
    l_new = alpha * l_prev + jnp.sum(p, axis=1, keepdims=True)
    acc_new = alpha * acc_prev + _dot(p.astype(BF16), v)
    return m_new, l_new, acc_new


FOX_DEAD = -160.0


def _fox_prompt_kernel(ck_end_ref, q_ref, k_ref, v_ref, cq_ref, ck_ref, o_ref, m_ref, l_ref, acc_ref, kmax_ref, *, tq):
    hi = pl.program_id(0)
    qi = pl.program_id(1)
    nq = pl.num_programs(1)

    @pl.when(qi == 0)
    def _():
        def norm_body(j, best):
            kb = k_ref[pl.ds(pl.multiple_of(j * tq, tq), tq), :].astype(F32)
            return jnp.maximum(best, jnp.sum(kb * kb, axis=1, keepdims=True))

        best = lax.fori_loop(0, nq, norm_body, jnp.zeros((tq, 1), F32))
        kmax_ref[...] = jnp.broadcast_to(jnp.sqrt(jnp.max(best, axis=0, keepdims=True)), kmax_ref.shape)

    m_ref[...] = jnp.full_like(m_ref, -jnp.inf)
    l_ref[...] = jnp.zeros_like(l_ref)
    acc_ref[...] = jnp.zeros_like(acc_ref)
    q = q_ref[...]
    qf = q.astype(F32)
    reach = jnp.sqrt(jnp.sum(qf * qf, axis=1, keepdims=True)) * kmax_ref[0:1, 0:1] + cq_ref[:, 0:1]

    def update(j, nblk, causal):
        width = nblk * tq
        rows = pl.ds(pl.multiple_of(j * tq, tq), width)
        ck = ck_ref[j] if nblk == 1 else jnp.concatenate([ck_ref[j + i] for i in range(nblk)], axis=1)
        s2 = _dot_nt(q, k_ref[rows, :]) + _tile_lanes(cq_ref[...], width) - ck
        if causal:
            row = lax.broadcasted_iota(jnp.int32, (tq, tq), 0)
            col = lax.broadcasted_iota(jnp.int32, (tq, tq), 1)
            tail = jnp.where(col <= row, s2[:, width - tq:], -jnp.inf)
            s2 = tail if nblk == 1 else jnp.concatenate([s2[:, :width - tq], tail], axis=1)
        m_ref[...], l_ref[...], acc_ref[...] = _fox_update(s2, v_ref[rows, :], m_ref[...], l_ref[...], acc_ref[...])

    def live(j_last):
        return jnp.max(reach - m_ref[:, 0:1]) - ck_end_ref[hi, j_last] > FOX_DEAD

    @pl.when(qi == 0)
    def _():
        update(0, 1, True)

    @pl.when(qi >= 1)
    def _():
        update(qi - 1, 2, True)

    n_rest = jnp.maximum(qi - 1, 0)

    def cond(state):
        i, alive = state
        return jnp.logical_and(i < n_rest // 4, alive)

    def body(state):
        i, _ = state
        update(n_rest - 4 - 4 * i, 4, False)
        return i + 1, live(jnp.maximum(n_rest - 5 - 4 * i, 0))

    lax.while_loop(cond, body, (0, live(jnp.maximum(n_rest - 1, 0))))
    rem = n_rest % 4

    @pl.when(jnp.logical_and(rem >= 2, live(jnp.maximum(rem - 1, 0))))
    def _():
        update(rem % 2, 2, False)

    @pl.when(jnp.logical_and(rem % 2 == 1, live(0)))
    def _():
        update(0, 1, False)

    o_ref[...] = (acc_ref[...] / l_ref[...]).astype(o_ref.dtype)


def fox_prompt(q, k, v, cq, ck, ck_end, tq=512):
    t, hd = q.shape
    h = hd // HEAD_DIM
    tq = min(tq, t)
    nq = t // tq
    return pl.pallas_call(
        functools.partial(_fox_prompt_kernel, tq=tq),
        grid=(h, nq),
        in_specs=[
            pl.BlockSpec(memory_space=pltpu.SMEM),
            pl.BlockSpec((tq, HEAD_DIM), lambda hi, qi: (qi, hi)),
            pl.BlockSpec((t, HEAD_DIM), lambda hi, qi: (0, hi)),
            pl.BlockSpec((t, HEAD_DIM), lambda hi, qi: (0, hi)),
            pl.BlockSpec((None, tq, LANES), lambda hi, qi: (hi, qi, 0)),
            pl.BlockSpec((None, nq, 1, tq), lambda hi, qi: (hi, 0, 0, 0)),
        ],
        out_specs=pl.BlockSpec((tq, HEAD_DIM), lambda hi, qi: (qi, hi)),
        out_shape=jax.ShapeDtypeStruct((t, hd), BF16),
        scratch_shapes=[pltpu.VMEM((tq, LANES), F32), pltpu.VMEM((tq, LANES), F32),
                        pltpu.VMEM((tq, HEAD_DIM), F32), pltpu.VMEM((8, LANES), F32)],
        compiler_params=_cparams(("arbitrary", "arbitrary")),
        name="fox_prompt",
    )(ck_end, q, k, v, cq, ck)


def _fox_decode_kernel(q_ref, kn_ref, vn_ref, kc_ref, vc_ref, cq_ref, ckc_ref, ckn_ref, o_ref,
                       m_ref, l_ref, acc_ref, *, heads, tk):
    kb = pl.program_id(1)
    tq = q_ref.shape[0]
    n_new = kn_ref.shape[0]

    @pl.when(kb == 0)
    def _():
        row = lax.broadcasted_iota(jnp.int32, (tq, n_new), 0)
        col = lax.broadcasted_iota(jnp.int32, (tq, n_new), 1)
        for h in range(heads):
            hc = _head_cols(h)
            s2 = _dot_nt(q_ref[:, hc], kn_ref[:, hc]) + _tile_lanes(cq_ref[h], n_new) - ckn_ref[h]
            s2 = jnp.where(col <= row, s2, -jnp.inf)
            m = jnp.max(s2, axis=1, keepdims=True)
            p = jnp.exp2(s2 - m)
            m_ref[h] = jnp.broadcast_to(m, (tq, LANES))
            l_ref[h] = jnp.broadcast_to(jnp.sum(p, axis=1, keepdims=True), (tq, LANES))
            acc_ref[h] = _dot(p.astype(BF16), vn_ref[:, hc])

    for h in range(heads):
        k_h = _head_rows(kc_ref, h, heads, tk).astype(BF16)
        v_h = _head_rows(vc_ref, h, heads, tk).astype(BF16)
        s2 = _dot_nt(q_ref[:, _head_cols(h)], k_h) + _tile_lanes(cq_ref[h], tk) - ckc_ref[h]
        m_ref[h], l_ref[h], acc_ref[h] = _fox_update(s2, v_h, m_ref[h], l_ref[h], acc_ref[h])

    @pl.when(kb == pl.num_programs(1) - 1)
    def _():
        for h in range(heads):
            o_ref[:, _head_cols(h)] = (acc_ref[h] / l_ref[h]).astype(o_ref.dtype)


def fox_decode(q, k_new, v_new, k_cache, v_cache, cq, ck_cache, ck_new, heads, tk=1024):
    b, tq, hd = q.shape
    n_new = k_new.shape[1]
    p_len = k_cache.shape[1] // heads
    tk = min(tk, p_len)
    return pl.pallas_call(
        functools.partial(_fox_decode_kernel, heads=heads, tk=tk),
        grid=(b, p_len // tk),
        in_specs=[
            pl.BlockSpec((None, tq, hd), lambda bi, kb: (bi, 0, 0)),
            pl.BlockSpec((None, n_new, hd), lambda bi, kb: (bi, 0, 0)),
            pl.BlockSpec((None, n_new, hd), lambda bi, kb: (bi, 0, 0)),
            pl.BlockSpec((None, tk * heads, HEAD_DIM), lambda bi, kb: (bi, kb, 0)),
            pl.BlockSpec((None, tk * heads, HEAD_DIM), lambda bi, kb: (bi, kb, 0)),
            pl.BlockSpec((None, heads, tq, LANES), lambda bi, kb: (bi, 0, 0, 0)),
            pl.BlockSpec((None, heads, 1, tk), lambda bi, kb: (bi, 0, 0, kb)),
            pl.BlockSpec((None, heads, 1, n_new), lambda bi, kb: (bi, 0, 0, 0)),
        ],
        out_specs=pl.BlockSpec((None, tq, hd), lambda bi, kb: (bi, 0, 0)),
        out_shape=jax.ShapeDtypeStruct((b, tq, hd), BF16),
        scratch_shapes=[pltpu.VMEM((heads, tq, LANES), F32), pltpu.VMEM((heads, tq, LANES), F32),
                        pltpu.VMEM((heads, tq, HEAD_DIM), F32)],
        compiler_params=_cparams(("parallel", "arbitrary")),
        name="fox_decode",
    )(q, k_new, v_new, k_cache, v_cache, cq, ck_cache, ck_new)


SB_DEAD = -160.0


def _suffix_sums(sp, ntri, carry):
    width = ntri.shape[0]
    n = sp.shape[1]
    parts = []
    for c0 in range(n - width, -1, -width):
        piece = sp[:, c0:c0 + width]
        hi = piece.astype(BF16)
        lo = (piece - hi.astype(F32)).astype(BF16)
        c = _dot(hi, ntri) + _dot(lo, ntri) + carry
        carry = c[:, 0:1]
        parts.append(c)
    parts.reverse()
    return (parts[0] if len(parts) == 1 else jnp.concatenate(parts, axis=1)), carry


def _mask_tail(x, mask):
    if mask is None:
        return x
    head = x.shape[1] - mask.shape[1]
    tail = jnp.where(mask, x[:, head:], 0.0)
    return tail if head == 0 else jnp.concatenate([x[:, :head], tail], axis=1)


def _sb_weights(z2, ntri, carry, mask):
    sp = _mask_tail(_softplus2(z2), mask)
    c, new_carry = _suffix_sums(sp, ntri, carry)
    a = _mask_tail(jnp.exp2(z2 + c), mask)
    return a.astype(BF16), new_carry


def _sb_prompt_kernel(q_ref, k_ref, v_ref, ntri_ref, o_ref, r_ref, acc_ref, *, tq):
    qi = pl.program_id(1)
    q = q_ref[...]
    ntri = ntri_ref[...]

    def block(j, nblk, carry, mask):
        rows = pl.ds(pl.multiple_of(j * tq, tq), nblk * tq)
        a, carry = _sb_weights(_dot_nt(q, k_ref[rows, :]), ntri, carry, mask)
        return _dot(a, v_ref[rows, :]), carry

    row = lax.broadcasted_iota(jnp.int32, (tq, tq), 0)
    col = lax.broadcasted_iota(jnp.int32, (tq, tq), 1)

    def first(j, nblk):
        out, carry = block(j, nblk, jnp.zeros((tq, 1), F32), col < row)
        acc_ref[...] = out
        r_ref[...] = jnp.broadcast_to(carry, r_ref.shape)

    @pl.when(qi == 0)
    def _():
        first(0, 1)

    @pl.when(qi >= 1)
    def _():
        first(qi - 1, 2)

    def step(j, nblk):
        out, carry = block(j, nblk, r_ref[:, 0:1], None)
        acc_ref[...] += out
        r_ref[...] = jnp.broadcast_to(carry, r_ref.shape)

    def live():
        return jnp.max(r_ref[...]) > SB_DEAD

    n_rest = jnp.maximum(qi - 1, 0)

    def cond(state):
        i, alive = state
        return jnp.logical_and(i < n_rest // 4, alive)

    def body(state):
        i, _ = state
        step(n_rest - 4 - 4 * i, 4)
        return i + 1, live()

    lax.while_loop(cond, body, (0, live()))

    @pl.when(jnp.logical_and(n_rest % 4 >= 2, live()))
    def _():
        step(n_rest % 2, 2)

    @pl.when(jnp.logical_and(n_rest % 2 == 1, live()))
    def _():
        step(0, 1)

    o_ref[...] = acc_ref[...].astype(o_ref.dtype)


def sb_prompt(q, k, v, ntri, tq=512):
    t, hd = q.shape
    h = hd // HEAD_DIM
    tq = min(tq, t)
    return pl.pallas_call(
        functools.partial(_sb_prompt_kernel, tq=tq),
        grid=(h, t // tq),
        in_specs=[
            pl.BlockSpec((tq, HEAD_DIM), lambda hi, qi: (qi, hi)),
            pl.BlockSpec((t, HEAD_DIM), lambda hi, qi: (0, hi)),
            pl.BlockSpec((t, HEAD_DIM), lambda hi, qi: (0, hi)),
            _resident(ntri.shape, lambda hi, qi: (0, 0)),
        ],
        out_specs=pl.BlockSpec((tq, HEAD_DIM), lambda hi, qi: (qi, hi)),
        out_shape=jax.ShapeDtypeStruct((t, hd), BF16),
        scratch_shapes=[pltpu.VMEM((tq, LANES), F32), pltpu.VMEM((tq, HEAD_DIM), F32)],
        compiler_params=_cparams(("parallel", "arbitrary")),
        name="sb_prompt",
    )(q, k, v, ntri)


def _sb_decode_kernel(q_ref, kn_ref, vn_ref, kc_hbm, vc_hbm, ntri_ref, o_ref, kbuf, vbuf, sem, r_ref, acc_ref,
                      *, heads, tk):
    b = pl.program_id(0)
    tq = q_ref.shape[0]
    n_new = kn_ref.shape[0]
    blk = tk * heads
    nkb = kc_hbm.shape[1] // blk

    def fetch(j, slot):
        rows = pl.ds(pl.multiple_of(j * blk, blk), blk)
        return (pltpu.make_async_copy(kc_hbm.at[b, rows, :], kbuf.at[slot], sem.at[0, slot]),
                pltpu.make_async_copy(vc_hbm.at[b, rows, :], vbuf.at[slot], sem.at[1, slot]))

    for copy in fetch(nkb - 1, 0):
        copy.start()

    def head_rows(h):
        return slice(h * tq, (h + 1) * tq)

    def stacked_scores(keys_of):
        return jnp.concatenate([_dot_nt(q_ref[:, _head_cols(h)], keys_of(h)) for h in range(heads)], axis=0)

    def live():
        return jnp.max(r_ref[...]) > SB_DEAD

    z2 = stacked_scores(lambda h: kn_ref[:, _head_cols(h)])
    row = lax.rem(lax.broadcasted_iota(jnp.int32, (heads * tq, n_new), 0), tq)
    col = lax.broadcasted_iota(jnp.int32, (heads * tq, n_new), 1)
    a, carry = _sb_weights(z2, ntri_ref[0:n_new, 0:n_new], jnp.zeros((heads * tq, 1), F32), col < row)
    for h in range(heads):
        acc_ref[head_rows(h), :] = _dot(a[head_rows(h), :], vn_ref[:, _head_cols(h)])
    r_ref[...] = jnp.broadcast_to(carry, r_ref.shape)

    def body(state):
        i, _ = state
        j = nkb - 1 - i
        slot = lax.rem(i, 2)
        for copy in fetch(j, slot):
            copy.wait()

        @pl.when(j > 0)
        def _():
            for copy in fetch(j - 1, 1 - slot):
                copy.start()

        z2 = stacked_scores(lambda h: _head_rows(kbuf.at[slot], h, heads, tk).astype(BF16))
        a, carry = _sb_weights(z2, ntri_ref[...], r_ref[:, 0:1], None)
        for h in range(heads):
            acc_ref[head_rows(h), :] += _dot(a[head_rows(h), :], _head_rows(vbuf.at[slot], h, heads, tk).astype(BF16))
        r_ref[...] = jnp.broadcast_to(carry, r_ref.shape)
        return i + 1, live()

    n_done, _ = lax.while_loop(lambda st: jnp.logical_and(st[0] < nkb, st[1]), body, (0, live()))

    @pl.when(n_done < nkb)
    def _():
        for copy in fetch(nkb - 1 - n_done, lax.rem(n_done, 2)):
            copy.wait()

    for h in range(heads):
        o_ref[:, _head_cols(h)] = acc_ref[head_rows(h), :].astype(o_ref.dtype)


def sb_decode(q, k_new, v_new, k_cache, v_cache, ntri, heads, tk=512):
    b, tq, hd = q.shape
    n_new = k_new.shape[1]
    p_len = k_cache.shape[1] // heads
    tk = min(tk, p_len)
    return pl.pallas_call(
        functools.partial(_sb_decode_kernel, heads=heads, tk=tk),
        grid=(b,),
        in_specs=[
            pl.BlockSpec((None, tq, hd), lambda bi: (bi, 0, 0)),
            pl.BlockSpec((None, n_new, hd), lambda bi: (bi, 0, 0)),
            pl.BlockSpec((None, n_new, hd), lambda bi: (bi, 0, 0)),
            pl.BlockSpec(memory_space=pl.ANY),
            pl.BlockSpec(memory_space=pl.ANY),
            _resident(ntri.shape, lambda bi: (0, 0)),
        ],
        out_specs=pl.BlockSpec((None, tq, hd), lambda bi: (bi, 0, 0)),
        out_shape=jax.ShapeDtypeStruct((b, tq, hd), BF16),
        scratch_shapes=[pltpu.VMEM((2, tk * heads, HEAD_DIM), F32), pltpu.VMEM((2, tk * heads, HEAD_DIM), F32),
                        pltpu.SemaphoreType.DMA((2, 2)),
                        pltpu.VMEM((heads * tq, LANES), F32), pltpu.VMEM((heads * tq, HEAD_DIM), F32)],
        compiler_params=_cparams(("arbitrary",)),
        name="sb_decode",
    )(q, k_new, v_new, k_cache, v_cache, ntri)


def _out_proj_kernel(xa_ref, xb_ref, w_ref, h_ref, o_ref):
    ka = xa_ref.shape[1]
    o_ref[...] = h_ref[...] + _dot(xa_ref[...], w_ref[:ka, :]) + _dot(xb_ref[...], w_ref[ka:, :])


def out_proj(xa, xb, col_b, w, h, tm=512):
    m, d = h.shape
    ka = w.shape[0] // 2
    tm = min(tm, m)
    return pl.pallas_call(
        _out_proj_kernel,
        grid=(m // tm,),
        in_specs=[
            pl.BlockSpec((tm, ka), lambda i: (i, 0)),
            pl.BlockSpec((tm, ka), lambda i: (i, col_b)),
            _resident(w.shape, lambda i: (0, 0)),
            pl.BlockSpec((tm, d), lambda i: (i, 0)),
        ],
        out_specs=pl.BlockSpec((tm, d), lambda i: (i, 0)),
        out_shape=jax.ShapeDtypeStruct((m, d), F32),
        compiler_params=_cparams(("parallel",)),
        name="out_proj",
    )(xa, xb, w, h)


def _ffn_kernel(h_ref, g_ref, wg_ref, wu_ref, wd_ref, gf_ref, o_ref, xn_ref, *, final_norm):
    fi = pl.program_id(1)

    @pl.when(fi == 0)
    def _():
        xn_ref[...] = _rms(h_ref[...], g_ref[...]).astype(BF16)
        o_ref[...] = jnp.zeros_like(o_ref)

    xn = xn_ref[...]
    gate = _dot(xn, wg_ref[...])
    up = _dot(xn, wu_ref[...])
    mid = (gate / (1.0 + jnp.exp(-gate)) * up).astype(BF16)
    width = wd_ref.shape[0]
    for c0 in range(0, o_ref.shape[1], width):
        o_ref[:, c0:c0 + width] += _dot(mid, wd_ref[:, c0:c0 + width])

    @pl.when(fi == pl.num_programs(1) - 1)
    def _():
        hn = h_ref[...] + o_ref[...]
        o_ref[...] = _rms(hn, gf_ref[...]) if final_norm else hn


def ffn(h, g, w_gate, w_up, w_down, layer, g_final, final_norm, tm=1024, tf=512):
    m, d = h.shape
    f = w_gate.shape[2]
    tm = min(tm, m)
    return pl.pallas_call(
        functools.partial(_ffn_kernel, final_norm=final_norm),
        grid=(m // tm, f // tf),
        in_specs=[
            pl.BlockSpec((tm, d), lambda i, j: (i, 0), pipeline_mode=pl.Buffered(1)),
            _resident((1, d), lambda i, j: (0, 0)),
            pl.BlockSpec((None, d, tf), lambda i, j: (layer, 0, j)),
            pl.BlockSpec((None, d, tf), lambda i, j: (layer, 0, j)),
            pl.BlockSpec((None, tf, d), lambda i, j: (layer, j, 0)),
            _resident((1, d), lambda i, j: (0, 0)),
        ],
        out_specs=pl.BlockSpec((tm, d), lambda i, j: (i, 0)),
        out_shape=jax.ShapeDtypeStruct((m, d), F32),
        scratch_shapes=[pltpu.VMEM((tm, d), BF16)],
        compiler_params=_cparams(("parallel", "arbitrary")),
        name="ffn",
    )(h, g, w_gate, w_up, w_down, g_final)


def _pad_rows(x, rows):
    return jnp.pad(x, ((0, 0), (0, rows - x.shape[1]), (0, 0)))


def _neg_tri_matrix():
    idx = np.arange(TRI)
    return jnp.asarray(-(idx[:, None] >= idx[None, :]).astype(np.float32), dtype=BF16)


def _layer0_groups(pool_w, fox_w):
    outs = [("f32",), ("qscaled",), ("f32", "bf16"), ("f32", "bf16")]
    groups, out_defs, col = [], [], 0
    for kinds in outs:
        width = pool_w if col == 0 else fox_w
        o = []
        for kind in kinds:
            o.append((len(out_defs), 0, kind))
            out_defs.append((width, F32 if kind == "f32" else BF16))
        groups.append((col, width, tuple(o)))
        col += width
    groups.append((None, LANES, ((len(out_defs), 0, "logsig"),)))
    out_defs.append((LANES, F32))
    return tuple(groups), out_defs


def _layer1_groups(d):
    half = d // 2
    groups = []
    for part, outs in enumerate([(0,), (1, 2), (3, 4)]):
        for c in range(2):
            kinds = ("qscaled",) if part == 0 else ("f32", "bf16")
            groups.append((part * d + c * half, half,
                           tuple((oi, c * half, kind) for oi, kind in zip(outs, kinds))))
    out_defs = [(d, BF16), (d, F32), (d, BF16), (d, F32), (d, BF16)]
    return tuple(groups), out_defs


def _run_trunk(x, n_past, pool_c, fk_c, fv_c, flf_c, sk_c, sv_c, wts):
    b, t, d = x.shape
    m = b * t
    h = x.reshape(m, d)
    pool_w = wts["w_pool"].shape[0] * wts["w_pool"].shape[1]
    fox_w = d - pool_w
    fox_h = fox_w // HEAD_DIM
    sb_h = d // HEAD_DIM
    first = fk_c is None
    zero_bias = jnp.zeros((1, LANES), F32)

    groups, out_defs = _layer0_groups(pool_w, fox_w)
    u, q, k, kb, v, vb, lf = norm_proj(h, wts["ln_mix0"], wts["w_in"], wts["w_forget"], wts["b_forget"], groups,
                                        out_defs, tm=256)
    log_f = lf[:, :fox_h]
    if first:
        prefix = jnp.zeros((b, POOL_HALO, pool_w), F32)
    else:
        prefix = jnp.pad(pool_c, ((0, 0), (POOL_HALO - POOL_STATE, 0), (0, 0)))
    y_pool = pool_mixer(u.reshape(b, t, pool_w), prefix, wts["w_pool"], wts["pool_scale"], n_past)
    if t >= POOL_STATE:
        pool_state = u.reshape(b, t, pool_w)[:, t - POOL_STATE:]
    else:
        pool_state = jnp.concatenate([prefix[:, 1:], u.reshape(b, t, pool_w)], axis=1)[:, -POOL_STATE:]

    if first:
        tq = min(512, t)
        cum2 = cumsum_rows(lf)[:, :fox_h].T * LOG2E
        cq = jnp.broadcast_to(cum2[:, :, None], (fox_h, t, LANES))
        ck = cum2.reshape(fox_h, t // tq, 1, tq)
        o = fox_prompt(q, kb, vb, cq, ck, ck[:, :, 0, tq - 1], tq=tq)
    else:
        n_new = LANES
        lf_new = log_f.reshape(b, t, fox_h).transpose(1, 0, 2).reshape(t, b * fox_h)
        lf_all = jnp.concatenate([flf_c.transpose(1, 0, 2).reshape(n_past, b * fox_h), lf_new], axis=0)
        rows = n_past + t
        rows_pad = -(-rows // 512) * 512
        lf_all = jnp.pad(lf_all, ((0, rows_pad - rows), (0, LANES - b * fox_h)))
        cum2 = cumsum_rows(lf_all)[:rows, :b * fox_h].reshape(rows, b, fox_h).transpose(1, 2, 0) * LOG2E
        ck_cache = cum2[:, :, None, :n_past]
        cum_new = cum2[:, :, n_past:]
        ck_new = jnp.pad(cum_new, ((0, 0), (0, 0), (0, n_new - t)))[:, :, None, :]
        cq = jnp.broadcast_to(cum_new[:, :, :, None], (b, fox_h, t, LANES))
        o = fox_decode(q.reshape(b, t, fox_w), _pad_rows(kb.reshape(b, t, fox_w), n_new),
                       _pad_rows(vb.reshape(b, t, fox_w), n_new), fk_c, fv_c, cq, ck_cache, ck_new, fox_h)
        o = o.reshape(m, fox_w)
    h = out_proj(y_pool.reshape(m, pool_w), o, 0, wts["w_out_ab"], h)
    h = ffn(h, wts["ln_ffn0"], wts["w_gate"], wts["w_up"], wts["w_down"], 0, wts["ln_final"], False)

    groups1, out_defs1 = _layer1_groups(d)
    q1, k1, k1b, v1, v1b = norm_proj(h, wts["ln_mix1"], wts["w_qkv"], wts["w_forget"], zero_bias, groups1, out_defs1,
                                     tm=256)
    ntri = _neg_tri_matrix()
    if first:
        o1 = sb_prompt(q1, k1b, v1b, ntri)
    else:
        n_new = LANES
        o1 = sb_decode(q1.reshape(b, t, d), _pad_rows(k1b.reshape(b, t, d), n_new),
                       _pad_rows(v1b.reshape(b, t, d), n_new), sk_c, sv_c, ntri, sb_h).reshape(m, d)
    h = out_proj(o1, o1, 1, wts["w_out_sb"], h)
    y = ffn(h, wts["ln_ffn1"], wts["w_gate"], wts["w_up"], wts["w_down"], 1, wts["ln_final"], True)

    return (y.reshape(b, t, d), pool_state[None], k.reshape(1, b, t, fox_h, HEAD_DIM),
            v.reshape(1, b, t, fox_h, HEAD_DIM), log_f.reshape(1, b, t, fox_h),
            k1.reshape(1, b, t, sb_h, HEAD_DIM), v1.reshape(1, b, t, sb_h, HEAD_DIM))


def kernel(x_prompt, x_sample, cache_pool, cache_fox_k, cache_fox_v, cache_fox_logf, cache_sb_k, cache_sb_v, ln_mix, w_in_ab, b_forget, w_pool, pool_scale, w_out_ab, w_qkv_sb, w_out_sb, ln_ffn, w_gate, w_up, w_down, ln_final):
    fox_h = b_forget.shape[-1]
    n_main = w_in_ab.shape[-1] - fox_h
    wts = {
        "ln_mix0": ln_mix[0][None], "ln_mix1": ln_mix[1][None],
        "ln_ffn0": ln_ffn[0][None], "ln_ffn1": ln_ffn[1][None], "ln_final": ln_final[None],
        "w_in": w_in_ab[0, :, :n_main].astype(BF16),
        "w_forget": jnp.pad(w_in_ab[0, :, n_main:], ((0, 0), (0, LANES - fox_h))).astype(BF16),
        "b_forget": jnp.pad(b_forget[0], (0, LANES - fox_h))[None],
        "w_pool": w_pool[0].astype(BF16), "pool_scale": pool_scale[0][None],
        "w_out_ab": w_out_ab[0].astype(BF16), "w_qkv": w_qkv_sb[0].astype(BF16),
        "w_out_sb": w_out_sb[0].astype(BF16),
        "w_gate": w_gate.astype(BF16), "w_up": w_up.astype(BF16), "w_down": w_down.astype(BF16),
    }
    out_p = _run_trunk(x_prompt, 0, None, None, None, None, None, None, wts)
    n_past = cache_fox_k.shape[2]
    bs = x_sample.shape[0]

    def rows_of(cache):
        return cache.reshape(bs, -1, HEAD_DIM)

    out_s = _run_trunk(x_sample, n_past, cache_pool[0], rows_of(cache_fox_k), rows_of(cache_fox_v),
                       cache_fox_logf[0], rows_of(cache_sb_k), rows_of(cache_sb_v), wts)
    return (out_p[0], out_s[0]) + out_p[1:] + out_s[1:]
```

```python
import functools

import numpy as np
import jax
import jax.numpy as jnp
from jax import lax
from jax.experimental import pallas as pl
from jax.experimental.pallas import tpu as pltpu

HEAD_DIM = 128
EPS = 1e-6
POOL_WINDOWS = (2, 4, 8, 16)
POOL_STATE = max(POOL_WINDOWS) - 1
POOL_HALO = 16
LANES = 128
TRI = 256
VMEM_LIMIT = 56 * 1024 * 1024
LOG2E = 1.4426950408889634
QK_SCALE2 = HEAD_DIM ** -0.5 * LOG2E

F32 = jnp.float32
BF16 = jnp.bfloat16


def _cparams(sem, vmem=VMEM_LIMIT):
    return pltpu.CompilerParams(dimension_semantics=sem, vmem_limit_bytes=vmem)


def _resident(shape, index_map):
    return pl.BlockSpec(shape, index_map, pipeline_mode=pl.Buffered(1))


def _dot(a, b):
    return jnp.dot(a, b, preferred_element_type=F32)


def _dot_nt(a, b):
    return lax.dot_general(a, b, (((1,), (1,)), ((), ())), preferred_element_type=F32)


def _rms(x, g):
    ms = jnp.mean(x * x, axis=-1, keepdims=True)
    return x * lax.rsqrt(ms + EPS) * g


def _softplus(z):
    return jnp.maximum(z, 0.0) + jnp.log1p(jnp.exp(-jnp.abs(z)))


def _softplus2(z2):
    return jnp.maximum(z2, 0.0) + jnp.log(1.0 + jnp.exp2(-jnp.abs(z2))) * LOG2E


def _tile_lanes(x, n):
    reps = n // LANES
    return x if reps == 1 else jnp.concatenate([x] * reps, axis=1)


def _head_cols(h):
    return slice(h * HEAD_DIM, (h + 1) * HEAD_DIM)


def _head_rows(ref, h, heads, n):
    return ref[pl.ds(h, n, stride=heads), :]


def _norm_proj_kernel(x_ref, g_ref, w_ref, w2_ref, b_ref, *out_refs, groups):
    xn = _rms(x_ref[...], g_ref[...]).astype(BF16)
    for w_col, width, outs in groups:
        acc = _dot(xn, w2_ref[...] if w_col is None else w_ref[:, w_col:w_col + width])
        for out_idx, out_col, kind in outs:
            if kind == "logsig":
                val = -_softplus(-(acc + b_ref[...]))
            elif kind == "qscaled":
                val = acc * QK_SCALE2
            else:
                val = acc
            o = out_refs[out_idx]
            o[:, out_col:out_col + width] = val.astype(o.dtype)


def norm_proj(x, g, w, w2, bias, groups, out_defs, tm):
    m, d = x.shape
    n = w.shape[1]
    tm = min(tm, m)
    out_shape = [jax.ShapeDtypeStruct((m, c), dt) for c, dt in out_defs]
    out_specs = [pl.BlockSpec((tm, c), lambda i: (i, 0)) for c, _ in out_defs]
    return pl.pallas_call(
        functools.partial(_norm_proj_kernel, groups=groups),
        grid=(m // tm,),
        in_specs=[
            pl.BlockSpec((tm, d), lambda i: (i, 0)),
            _resident((1, d), lambda i: (0, 0)),
            _resident((d, n), lambda i: (0, 0)),
            _resident((d, LANES), lambda i: (0, 0)),
            _resident((1, LANES), lambda i: (0, 0)),
        ],
        out_specs=out_specs,
        out_shape=out_shape,
        compiler_params=_cparams(("parallel",)),
        name="norm_proj",
    )(x, g, w, w2, bias)


def _cumsum_kernel(x_ref, o_ref, carry_ref, *, tb):
    @pl.when(pl.program_id(0) == 0)
    def _():
        carry_ref[...] = jnp.zeros_like(carry_ref)

    x = x_ref[...]
    row = lax.broadcasted_iota(jnp.int32, (tb, tb), 0)
    col = lax.broadcasted_iota(jnp.int32, (tb, tb), 1)
    low = jnp.where(col <= row, 1.0, 0.0).astype(BF16)
    hi = x.astype(BF16)
    r1 = x - hi.astype(F32)
    mid = r1.astype(BF16)
    lo = (r1 - mid.astype(F32)).astype(BF16)
    out = _dot(low, hi) + _dot(low, mid) + _dot(low, lo) + carry_ref[0:1, :]
    o_ref[...] = out
    carry_ref[...] = jnp.broadcast_to(out[tb - 1:tb, :], carry_ref.shape)


def cumsum_rows(x, tb=512):
    t, c = x.shape
    tb = min(tb, t)
    return pl.pallas_call(
        functools.partial(_cumsum_kernel, tb=tb),
        grid=(t // tb,),
        in_specs=[pl.BlockSpec((tb, c), lambda i: (i, 0))],
        out_specs=pl.BlockSpec((tb, c), lambda i: (i, 0)),
        out_shape=jax.ShapeDtypeStruct((t, c), F32),
        scratch_shapes=[pltpu.VMEM((8, c), F32)],
        compiler_params=_cparams(("arbitrary",)),
        name="cumsum_rows",
    )(x)


def _pool_kernel(u_ref, pre_ref, w_ref, sc_ref, o_ref, buf_ref, *, tm, pos0, group):
    mi = pl.program_id(1)

    @pl.when(mi == 0)
    def _():
        buf_ref[0:POOL_HALO, :] = pre_ref[...]

    @pl.when(mi > 0)
    def _():
        buf_ref[0:POOL_HALO, :] = buf_ref[tm:tm + POOL_HALO, :]

    buf_ref[POOL_HALO:POOL_HALO + tm, :] = u_ref[...]
    pos = pos0 + mi * tm + lax.broadcasted_iota(jnp.int32, (tm, 1), 0)
    for gi, win in enumerate(POOL_WINDOWS):
        cs = slice(gi * group, (gi + 1) * group)
        tot = buf_ref[POOL_HALO:POOL_HALO + tm, cs]
        for back in range(1, win):
            tot = tot + buf_ref[POOL_HALO - back:POOL_HALO - back + tm, cs]
        count = jnp.minimum(pos + 1, win).astype(F32)
        diff = tot / count - u_ref[:, cs]
        y = _dot(diff.astype(BF16), w_ref[gi]) * sc_ref[:, cs]
        o_ref[:, cs] = y.astype(o_ref.dtype)


def pool_mixer(u, prefix, w_pool, scale, pos0, tm=512):
    b, t, c = u.shape
    tm = min(tm, t)
    group = c // len(POOL_WINDOWS)
    return pl.pallas_call(
        functools.partial(_pool_kernel, tm=tm, pos0=pos0, group=group),
        grid=(b, t // tm),
        in_specs=[
            pl.BlockSpec((None, tm, c), lambda bi, mi: (bi, mi, 0)),
            pl.BlockSpec((None, POOL_HALO, c), lambda bi, mi: (bi, 0, 0)),
            _resident(w_pool.shape, lambda bi, mi: (0, 0, 0)),
            _resident((1, c), lambda bi, mi: (0, 0)),
        ],
        out_specs=pl.BlockSpec((None, tm, c), lambda bi, mi: (bi, mi, 0)),
        out_shape=jax.ShapeDtypeStruct((b, t, c), BF16),
        scratch_shapes=[pltpu.VMEM((POOL_HALO + tm, c), F32)],
        compiler_params=_cparams(("arbitrary", "arbitrary")),
        name="pool_mixer",
    )(u, prefix, w_pool, scale)


def _fox_update(s2, v, m_prev, l_prev, acc_prev):
    m_new = jnp.maximum(m_prev, jnp.max(s2, axis=1, keepdims=True))
    alpha = jnp.exp2(m_prev - m_new)
    p = jnp.exp2(s2 - _tile_lanes(m_new, s2.shape[1]))
    l_new = alpha * l_prev + jnp.sum(p, axis=1, keepdims=True)
    acc_new = alpha * acc_prev + _dot(p.astype(BF16), v)
    return m_new, l_new, acc_new


FOX_DEAD = -160.0


def _fox_prompt_kernel(ck_end_ref, q_ref, k_ref, v_ref, cq_ref, ck_ref, o_ref, m_ref, l_ref, acc_ref, kmax_ref,
                       *, tq, bs):
    hi = pl.program_id(0)
    qi = pl.program_id(1)
    nq = pl.num_programs(1)
    nsub = tq // bs

    @pl.when(qi == 0)
    def _():
        def norm_body(j, best):
            kb = k_ref[pl.ds(pl.multiple_of(j * tq, tq), tq), :].astype(F32)
            return jnp.maximum(best, jnp.sum(kb * kb, axis=1, keepdims=True))

        best = lax.fori_loop(0, nq, norm_body, jnp.zeros((tq, 1), F32))
        kmax_ref[...] = jnp.broadcast_to(jnp.sqrt(jnp.max(best, axis=0, keepdims=True)), kmax_ref.shape)

    row = lax.broadcasted_iota(jnp.int32, (bs, bs), 0)
    col = lax.broadcasted_iota(jnp.int32, (bs, bs), 1)

    def sub_rows(s):
        return slice(s * bs, (s + 1) * bs)

    def update(s, j, nblk, causal, fresh=False):
        width = nblk * bs
        rows = pl.ds(pl.multiple_of(j * bs, bs), width)
        cq = cq_ref[sub_rows(s), :]
        ck = ck_ref[j] if nblk == 1 else jnp.concatenate([ck_ref[j + i] for i in range(nblk)], axis=1)
        s2 = _dot_nt(q_ref[sub_rows(s), :], k_ref[rows, :]) + _tile_lanes(cq, width) - ck
        if causal:
            tail = jnp.where(col <= row, s2[:, width - bs:], -jnp.inf)
            s2 = tail if nblk == 1 else jnp.concatenate([s2[:, :width - bs], tail], axis=1)
        if fresh:
            state = (jnp.full((bs, LANES), -jnp.inf, F32), jnp.zeros((bs, LANES), F32), jnp.zeros((bs, HEAD_DIM), F32))
        else:
            state = (m_ref[s], l_ref[s], acc_ref[s])
        m_ref[s], l_ref[s], acc_ref[s] = _fox_update(s2, v_ref[rows, :], *state)

    def first_units(at_origin):
        for s in range(nsub):
            if s == 0 and at_origin:
                update(s, 0, 1, True, fresh=True)
            else:
                update(s, qi * nsub + s - 1, 2, True, fresh=True)

    @pl.when(qi == 0)
    def _():
        first_units(True)

    @pl.when(qi > 0)
    def _():
        first_units(False)

    def rest(s):
        g = qi * nsub + s
        qf = q_ref[sub_rows(s), :].astype(F32)
        reach = (jnp.sqrt(jnp.sum(qf * qf, axis=1, keepdims=True)) * kmax_ref[0:1, 0:1]
                 + cq_ref[sub_rows(s), 0:1])

        def live(j_last):
            return jnp.max(reach - m_ref[s][:, 0:1]) - ck_end_ref[hi, j_last] > FOX_DEAD

        n_rest = jnp.maximum(g - 1, 0)

        def cond(state):
            i, alive = state
            return jnp.logical_and(i < n_rest // 4, alive)

        def body(state):
            i, _ = state
            update(s, n_rest - 4 - 4 * i, 4, False)
            return i + 1, live(jnp.maximum(n_rest - 5 - 4 * i, 0))

        lax.while_loop(cond, body, (0, live(jnp.maximum(n_rest - 1, 0))))
        rem = n_rest % 4

        @pl.when(jnp.logical_and(rem >= 2, live(jnp.maximum(rem - 1, 0))))
        def _():
            update(s, rem % 2, 2, False)

        @pl.when(jnp.logical_and(rem % 2 == 1, live(0)))
        def _():
            update(s, 0, 1, False)

        o_ref[sub_rows(s), :] = (acc_ref[s] / l_ref[s]).astype(o_ref.dtype)

    for s in range(nsub):
        rest(s)


def fox_prompt(q, k, v, cq, ck, ck_end, tq, bs):
    t, hd = q.shape
    h = hd // HEAD_DIM
    nq = t // tq
    return pl.pallas_call(
        functools.partial(_fox_prompt_kernel, tq=tq, bs=bs),
        grid=(h, nq),
        in_specs=[
            pl.BlockSpec(memory_space=pltpu.SMEM),
            pl.BlockSpec((tq, HEAD_DIM), lambda hi, qi: (qi, hi)),
            pl.BlockSpec((t, HEAD_DIM), lambda hi, qi: (0, hi)),
            pl.BlockSpec((t, HEAD_DIM), lambda hi, qi: (0, hi)),
            pl.BlockSpec((None, tq, LANES), lambda hi, qi: (hi, qi, 0)),
            pl.BlockSpec((None, t // bs, 1, bs), lambda hi, qi: (hi, 0, 0, 0)),
        ],
        out_specs=pl.BlockSpec((tq, HEAD_DIM), lambda hi, qi: (qi, hi)),
        out_shape=jax.ShapeDtypeStruct((t, hd), BF16),
        scratch_shapes=[pltpu.VMEM((tq // bs, bs, LANES), F32), pltpu.VMEM((tq // bs, bs, LANES), F32),
                        pltpu.VMEM((tq // bs, bs, HEAD_DIM), F32), pltpu.VMEM((8, LANES), F32)],
        compiler_params=_cparams(("arbitrary", "arbitrary")),
        name="fox_prompt",
    )(ck_end, q, k, v, cq, ck)


def _fox_decode_kernel(q_ref, kn_ref, vn_ref, kc_ref, vc_ref, cq_ref, ckc_ref, ckn_ref, o_ref,
                       m_ref, l_ref, acc_ref, *, heads, tk):
    kb = pl.program_id(1)
    tq = q_ref.shape[0]
    n_new = kn_ref.shape[0]

    @pl.when(kb == 0)
    def _():
        row = lax.broadcasted_iota(jnp.int32, (tq, n_new), 0)
        col = lax.broadcasted_iota(jnp.int32, (tq, n_new), 1)
        for h in range(heads):
            hc = _head_cols(h)
            s2 = _dot_nt(q_ref[:, hc], kn_ref[:, hc]) + _tile_lanes(cq_ref[h], n_new) - ckn_ref[h]
            s2 = jnp.where(col <= row, s2, -jnp.inf)
            m = jnp.max(s2, axis=1, keepdims=True)
            p = jnp.exp2(s2 - m)
            m_ref[h] = jnp.broadcast_to(m, (tq, LANES))
            l_ref[h] = jnp.broadcast_to(jnp.sum(p, axis=1, keepdims=True), (tq, LANES))
            acc_ref[h] = _dot(p.astype(BF16), vn_ref[:, hc])

    for h in range(heads):
        k_h = _head_rows(kc_ref, h, heads, tk).astype(BF16)
        v_h = _head_rows(vc_ref, h, heads, tk).astype(BF16)
        s2 = _dot_nt(q_ref[:, _head_cols(h)], k_h) + _tile_lanes(cq_ref[h], tk) - ckc_ref[h]
        m_ref[h], l_ref[h], acc_ref[h] = _fox_update(s2, v_h, m_ref[h], l_ref[h], acc_ref[h])

    @pl.when(kb == pl.num_programs(1) - 1)
    def _():
        for h in range(heads):
            o_ref[:, _head_cols(h)] = (acc_ref[h] / l_ref[h]).astype(o_ref.dtype)


def fox_decode(q, k_new, v_new, k_cache, v_cache, cq, ck_cache, ck_new, heads, tk=1024):
    b, tq, hd = q.shape
    n_new = k_new.shape[1]
    p_len = k_cache.shape[1] // heads
    tk = min(tk, p_len)
    return pl.pallas_call(
        functools.partial(_fox_decode_kernel, heads=heads, tk=tk),
        grid=(b, p_len // tk),
        in_specs=[
            pl.BlockSpec((None, tq, hd), lambda bi, kb: (bi, 0, 0)),
            pl.BlockSpec((None, n_new, hd), lambda bi, kb: (bi, 0, 0)),
            pl.BlockSpec((None, n_new, hd), lambda bi, kb: (bi, 0, 0)),
            pl.BlockSpec((None, tk * heads, HEAD_DIM), lambda bi, kb: (bi, kb, 0)),
            pl.BlockSpec((None, tk * heads, HEAD_DIM), lambda bi, kb: (bi, kb, 0)),
            pl.BlockSpec((None, heads, tq, LANES), lambda bi, kb: (bi, 0, 0, 0)),
            pl.BlockSpec((None, heads, 1, tk), lambda bi, kb: (bi, 0, 0, kb)),
            pl.BlockSpec((None, heads, 1, n_new), lambda bi, kb: (bi, 0, 0, 0)),
        ],
        out_specs=pl.BlockSpec((None, tq, hd), lambda bi, kb: (bi, 0, 0)),
        out_shape=jax.ShapeDtypeStruct((b, tq, hd), BF16),
        scratch_shapes=[pltpu.VMEM((heads, tq, LANES), F32), pltpu.VMEM((heads, tq, LANES), F32),
                        pltpu.VMEM((heads, tq, HEAD_DIM), F32)],
        compiler_params=_cparams(("parallel", "arbitrary")),
        name="fox_decode",
    )(q, k_new, v_new, k_cache, v_cache, cq, ck_cache, ck_new)


SB_DEAD = -160.0


def _suffix_sums(sp, ntri, carry):
    width = ntri.shape[0]
    n = sp.shape[1]
    parts = []
    for c0 in range(n - width, -1, -width):
        piece = sp[:, c0:c0 + width]
        hi = piece.astype(BF16)
        lo = (piece - hi.astype(F32)).astype(BF16)
        c = _dot(hi, ntri) + _dot(lo, ntri) + carry
        carry = c[:, 0:1]
        parts.append(c)
    parts.reverse()
    return (parts[0] if len(parts) == 1 else jnp.concatenate(parts, axis=1)), carry


def _mask_tail(x, mask):
    if mask is None:
        return x
    head = x.shape[1] - mask.shape[1]
    tail = jnp.where(mask, x[:, head:], 0.0)
    return tail if head == 0 else jnp.concatenate([x[:, :head], tail], axis=1)


def _sb_weights(z2, ntri, carry, mask):
    sp = _mask_tail(_softplus2(z2), mask)
    c, new_carry = _suffix_sums(sp, ntri, carry)
    a = _mask_tail(jnp.exp2(z2 + c), mask)
    return a.astype(BF16), new_carry


def _sb_prompt_kernel(q_ref, k_ref, v_ref, ntri_ref, o_ref, r_ref, acc_ref, *, tq, bs):
    qi = pl.program_id(1)
    nsub = tq // bs
    ntri = ntri_ref[...]
    row = lax.broadcasted_iota(jnp.int32, (bs, bs), 0)
    col = lax.broadcasted_iota(jnp.int32, (bs, bs), 1)

    def sub_rows(s):
        return slice(s * bs, (s + 1) * bs)

    def block(s, j, nblk, carry, mask):
        rows = pl.ds(pl.multiple_of(j * bs, bs), nblk * bs)
        a, carry = _sb_weights(_dot_nt(q_ref[sub_rows(s), :], k_ref[rows, :]), ntri, carry, mask)
        return _dot(a, v_ref[rows, :]), carry

    def first_units(at_origin):
        for s in range(nsub):
            if s == 0 and at_origin:
                out, carry = block(s, 0, 1, jnp.zeros((bs, 1), F32), col < row)
            else:
                out, carry = block(s, qi * nsub + s - 1, 2, jnp.zeros((bs, 1), F32), col < row)
            acc_ref[s] = out
            r_ref[s] = jnp.broadcast_to(carry, (bs, LANES))

    @pl.when(qi == 0)
    def _():
        first_units(True)

    @pl.when(qi > 0)
    def _():
        first_units(False)

    def rest(s):
        g = qi * nsub + s

        def step(j, nblk):
            out, carry = block(s, j, nblk, r_ref[s][:, 0:1], None)
            acc_ref[s] += out
            r_ref[s] = jnp.broadcast_to(carry, (bs, LANES))

        def live():
            return jnp.max(r_ref[s]) > SB_DEAD

        n_rest = jnp.maximum(g - 1, 0)

        def cond(state):
            i, alive = state
            return jnp.logical_and(i < n_rest // 4, alive)

        def body(state):
            i, _ = state
            step(n_rest - 4 - 4 * i, 4)
            return i + 1, live()

        lax.while_loop(cond, body, (0, live()))

        @pl.when(jnp.logical_and(n_rest % 4 >= 2, live()))
        def _():
            step(n_rest % 2, 2)

        @pl.when(jnp.logical_and(n_rest % 2 == 1, live()))
        def _():
            step(0, 1)

        o_ref[sub_rows(s), :] = acc_ref[s].astype(o_ref.dtype)

    for s in range(nsub):
        rest(s)


def sb_prompt(q, k, v, ntri, tq=512, bs=256):
    t, hd = q.shape
    h = hd // HEAD_DIM
    tq = min(tq, t)
    bs = min(bs, tq)
    return pl.pallas_call(
        functools.partial(_sb_prompt_kernel, tq=tq, bs=bs),
        grid=(h, t // tq),
        in_specs=[
            pl.BlockSpec((tq, HEAD_DIM), lambda hi, qi: (qi, hi)),
            pl.BlockSpec((t, HEAD_DIM), lambda hi, qi: (0, hi)),
            pl.BlockSpec((t, HEAD_DIM), lambda hi, qi: (0, hi)),
            _resident(ntri.shape, lambda hi, qi: (0, 0)),
        ],
        out_specs=pl.BlockSpec((tq, HEAD_DIM), lambda hi, qi: (qi, hi)),
        out_shape=jax.ShapeDtypeStruct((t, hd), BF16),
        scratch_shapes=[pltpu.VMEM((tq // bs, bs, LANES), F32), pltpu.VMEM((tq // bs, bs, HEAD_DIM), F32)],
        compiler_params=_cparams(("parallel", "arbitrary")),
        name="sb_prompt",
    )(q, k, v, ntri)


def _sb_decode_kernel(q_ref, kn_ref, vn_ref, kc_hbm, vc_hbm, ntri_ref, o_ref, kbuf, vbuf, sem, r_ref, acc_ref,
                      *, heads, tk):
    b = pl.program_id(0)
    tq = q_ref.shape[0]
    n_new = kn_ref.shape[0]
    blk = tk * heads
    nkb = kc_hbm.shape[1] // blk

    def fetch(j, slot):
        rows = pl.ds(pl.multiple_of(j * blk, blk), blk)
        return (pltpu.make_async_copy(kc_hbm.at[b, rows, :], kbuf.at[slot], sem.at[0, slot]),
                pltpu.make_async_copy(vc_hbm.at[b, rows, :], vbuf.at[slot], sem.at[1, slot]))

    for copy in fetch(nkb - 1, 0):
        copy.start()

    def head_rows(h):
        return slice(h * tq, (h + 1) * tq)

    def stacked_scores(keys_of):
        return jnp.concatenate([_dot_nt(q_ref[:, _head_cols(h)], keys_of(h)) for h in range(heads)], axis=0)

    def live():
        return jnp.max(r_ref[...]) > SB_DEAD

    z2 = stacked_scores(lambda h: kn_ref[:, _head_cols(h)])
    row = lax.rem(lax.broadcasted_iota(jnp.int32, (heads * tq, n_new), 0), tq)
    col = lax.broadcasted_iota(jnp.int32, (heads * tq, n_new), 1)
    a, carry = _sb_weights(z2, ntri_ref[0:n_new, 0:n_new], jnp.zeros((heads * tq, 1), F32), col < row)
    for h in range(heads):
        acc_ref[head_rows(h), :] = _dot(a[head_rows(h), :], vn_ref[:, _head_cols(h)])
    r_ref[...] = jnp.broadcast_to(carry, r_ref.shape)

    def body(state):
        i, _ = state
        j = nkb - 1 - i
        slot = lax.rem(i, 2)
        for copy in fetch(j, slot):
            copy.wait()

        @pl.when(j > 0)
        def _():
            for copy in fetch(j - 1, 1 - slot):
                copy.start()

        z2 = stacked_scores(lambda h: _head_rows(kbuf.at[slot], h, heads, tk).astype(BF16))
        a, carry = _sb_weights(z2, ntri_ref[...], r_ref[:, 0:1], None)
        for h in range(heads):
            acc_ref[head_rows(h), :] += _dot(a[head_rows(h), :], _head_rows(vbuf.at[slot], h, heads, tk).astype(BF16))
        r_ref[...] = jnp.broadcast_to(carry, r_ref.shape)
        return i + 1, live()

    n_done, _ = lax.while_loop(lambda st: jnp.logical_and(st[0] < nkb, st[1]), body, (0, live()))

    @pl.when(n_done < nkb)
    def _():
        for copy in fetch(nkb - 1 - n_done, lax.rem(n_done, 2)):
            copy.wait()

    for h in range(heads):
        o_ref[:, _head_cols(h)] = acc_ref[head_rows(h), :].astype(o_ref.dtype)


def sb_decode(q, k_new, v_new, k_cache, v_cache, ntri, heads, tk=512):
    b, tq, hd = q.shape
    n_new = k_new.shape[1]
    p_len = k_cache.shape[1] // heads
    tk = min(tk, p_len)
    return pl.pallas_call(
        functools.partial(_sb_decode_kernel, heads=heads, tk=tk),
        grid=(b,),
        in_specs=[
            pl.BlockSpec((None, tq, hd), lambda bi: (bi, 0, 0)),
            pl.BlockSpec((None, n_new, hd), lambda bi: (bi, 0, 0)),
            pl.BlockSpec((None, n_new, hd), lambda bi: (bi, 0, 0)),
            pl.BlockSpec(memory_space=pl.ANY),
            pl.BlockSpec(memory_space=pl.ANY),
            _resident(ntri.shape, lambda bi: (0, 0)),
        ],
        out_specs=pl.BlockSpec((None, tq, hd), lambda bi: (bi, 0, 0)),
        out_shape=jax.ShapeDtypeStruct((b, tq, hd), BF16),
        scratch_shapes=[pltpu.VMEM((2, tk * heads, HEAD_DIM), F32), pltpu.VMEM((2, tk * heads, HEAD_DIM), F32),
                        pltpu.SemaphoreType.DMA((2, 2)),
                        pltpu.VMEM((heads * tq, LANES), F32), pltpu.VMEM((heads * tq, HEAD_DIM), F32)],
        compiler_params=_cparams(("arbitrary",)),
        name="sb_decode",
    )(q, k_new, v_new, k_cache, v_cache, ntri)


def _out_proj_kernel(xa_ref, xb_ref, w_ref, h_ref, o_ref):
    ka = xa_ref.shape[1]
    o_ref[...] = h_ref[...] + _dot(xa_ref[...], w_ref[:ka, :]) + _dot(xb_ref[...], w_ref[ka:, :])


def out_proj(xa, xb, col_b, w, h, tm=512):
    m, d = h.shape
    ka = w.shape[0] // 2
    tm = min(tm, m)
    return pl.pallas_call(
        _out_proj_kernel,
        grid=(m // tm,),
        in_specs=[
            pl.BlockSpec((tm, ka), lambda i: (i, 0)),
            pl.BlockSpec((tm, ka), lambda i: (i, col_b)),
            _resident(w.shape, lambda i: (0, 0)),
            pl.BlockSpec((tm, d), lambda i: (i, 0)),
        ],
        out_specs=pl.BlockSpec((tm, d), lambda i: (i, 0)),
        out_shape=jax.ShapeDtypeStruct((m, d), F32),
        compiler_params=_cparams(("parallel",)),
        name="out_proj",
    )(xa, xb, w, h)


def _ffn_kernel(h_ref, g_ref, wg_ref, wu_ref, wd_ref, gf_ref, o_ref, xn_ref, *, final_norm):
    fi = pl.program_id(1)

    @pl.when(fi == 0)
    def _():
        xn_ref[...] = _rms(h_ref[...], g_ref[...]).astype(BF16)
        o_ref[...] = jnp.zeros_like(o_ref)

    xn = xn_ref[...]
    gate = _dot(xn, wg_ref[...])
    up = _dot(xn, wu_ref[...])
    mid = (gate / (1.0 + jnp.exp(-gate)) * up).astype(BF16)
    width = wd_ref.shape[0]
    for c0 in range(0, o_ref.shape[1], width):
        o_ref[:, c0:c0 + width] += _dot(mid, wd_ref[:, c0:c0 + width])

    @pl.when(fi == pl.num_programs(1) - 1)
    def _():
        hn = h_ref[...] + o_ref[...]
        o_ref[...] = _rms(hn, gf_ref[...]) if final_norm else hn


def ffn(h, g, w_gate, w_up, w_down, layer, g_final, final_norm, tm=1024, tf=512):
    m, d = h.shape
    f = w_gate.shape[2]
    tm = min(tm, m)
    return pl.pallas_call(
        functools.partial(_ffn_kernel, final_norm=final_norm),
        grid=(m // tm, f // tf),
        in_specs=[
            pl.BlockSpec((tm, d), lambda i, j: (i, 0), pipeline_mode=pl.Buffered(1)),
            _resident((1, d), lambda i, j: (0, 0)),
            pl.BlockSpec((None, d, tf), lambda i, j: (layer, 0, j)),
            pl.BlockSpec((None, d, tf), lambda i, j: (layer, 0, j)),
            pl.BlockSpec((None, tf, d), lambda i, j: (layer, j, 0)),
            _resident((1, d), lambda i, j: (0, 0)),
        ],
        out_specs=pl.BlockSpec((tm, d), lambda i, j: (i, 0)),
        out_shape=jax.ShapeDtypeStruct((m, d), F32),
        scratch_shapes=[pltpu.VMEM((tm, d), BF16)],
        compiler_params=_cparams(("parallel", "arbitrary")),
        name="ffn",
    )(h, g, w_gate, w_up, w_down, g_final)


def _pad_rows(x, rows):
    return jnp.pad(x, ((0, 0), (0, rows - x.shape[1]), (0, 0)))


def _neg_tri_matrix():
    idx = np.arange(TRI)
    return jnp.asarray(-(idx[:, None] >= idx[None, :]).astype(np.float32), dtype=BF16)


def _layer0_groups(pool_w, fox_w):
    outs = [("f32",), ("qscaled",), ("f32", "bf16"), ("f32", "bf16")]
    groups, out_defs, col = [], [], 0
    for kinds in outs:
        width = pool_w if col == 0 else fox_w
        o = []
        for kind in kinds:
            o.append((len(out_defs), 0, kind))
            out_defs.append((width, F32 if kind == "f32" else BF16))
        groups.append((col, width, tuple(o)))
        col += width
    groups.append((None, LANES, ((len(out_defs), 0, "logsig"),)))
    out_defs.append((LANES, F32))
    return tuple(groups), out_defs


def _layer1_groups(d):
    half = d // 2
    groups = []
    for part, outs in enumerate([(0,), (1, 2), (3, 4)]):
        for c in range(2):
            kinds = ("qscaled",) if part == 0 else ("f32", "bf16")
            groups.append((part * d + c * half, half,
                           tuple((oi, c * half, kind) for oi, kind in zip(outs, kinds))))
    out_defs = [(d, BF16), (d, F32), (d, BF16), (d, F32), (d, BF16)]
    return tuple(groups), out_defs


def _run_trunk(x, n_past, pool_c, fk_c, fv_c, flf_c, sk_c, sv_c, wts):
    b, t, d = x.shape
    m = b * t
    h = x.reshape(m, d)
    pool_w = wts["w_pool"].shape[0] * wts["w_pool"].shape[1]
    fox_w = d - pool_w
    fox_h = fox_w // HEAD_DIM
    sb_h = d // HEAD_DIM
    first = fk_c is None
    zero_bias = jnp.zeros((1, LANES), F32)

    groups, out_defs = _layer0_groups(pool_w, fox_w)
    u, q, k, kb, v, vb, lf = norm_proj(h, wts["ln_mix0"], wts["w_in"], wts["w_forget"], wts["b_forget"], groups,
                                        out_defs, tm=256)
    log_f = lf[:, :fox_h]
    if first:
        prefix = jnp.zeros((b, POOL_HALO, pool_w), F32)
    else:
        prefix = jnp.pad(pool_c, ((0, 0), (POOL_HALO - POOL_STATE, 0), (0, 0)))
    y_pool = pool_mixer(u.reshape(b, t, pool_w), prefix, wts["w_pool"], wts["pool_scale"], n_past)
    if t >= POOL_STATE:
        pool_state = u.reshape(b, t, pool_w)[:, t - POOL_STATE:]
    else:
        pool_state = jnp.concatenate([prefix[:, 1:], u.reshape(b, t, pool_w)], axis=1)[:, -POOL_STATE:]

    if first:
        tq = min(512, t)
        bs = min(256, tq)
        cum2 = cumsum_rows(lf)[:, :fox_h].T * LOG2E
        cq = jnp.broadcast_to(cum2[:, :, None], (fox_h, t, LANES))
        ck = cum2.reshape(fox_h, t // bs, 1, bs)
        o = fox_prompt(q, kb, vb, cq, ck, ck[:, :, 0, bs - 1], tq, bs)
    else:
        n_new = LANES
        lf_new = log_f.reshape(b, t, fox_h).transpose(1, 0, 2).reshape(t, b * fox_h)
        lf_all = jnp.concatenate([flf_c.transpose(1, 0, 2).reshape(n_past, b * fox_h), lf_new], axis=0)
        rows = n_past + t
        rows_pad = -(-rows // 512) * 512
        lf_all = jnp.pad(lf_all, ((0, rows_pad - rows), (0, LANES - b * fox_h)))
        cum2 = cumsum_rows(lf_all)[:rows, :b * fox_h].reshape(rows, b, fox_h).transpose(1, 2, 0) * LOG2E
        ck_cache = cum2[:, :, None, :n_past]
        cum_new = cum2[:, :, n_past:]
        ck_new = jnp.pad(cum_new, ((0, 0), (0, 0), (0, n_new - t)))[:, :, None, :]
        cq = jnp.broadcast_to(cum_new[:, :, :, None], (b, fox_h, t, LANES))
        o = fox_decode(q.reshape(b, t, fox_w), _pad_rows(kb.reshape(b, t, fox_w), n_new),
                       _pad_rows(vb.reshape(b, t, fox_w), n_new), fk_c, fv_c, cq, ck_cache, ck_new, fox_h)
        o = o.reshape(m, fox_w)
    h = out_proj(y_pool.reshape(m, pool_w), o, 0, wts["w_out_ab"], h)
    h = ffn(h, wts["ln_ffn0"], wts["w_gate"], wts["w_up"], wts["w_down"], 0, wts["ln_final"], False)

    groups1, out_defs1 = _layer1_groups(d)
    q1, k1, k1b, v1, v1b = norm_proj(h, wts["ln_mix1"], wts["w_qkv"], wts["w_forget"], zero_bias, groups1, out_defs1,
                                     tm=256)
    ntri = _neg_tri_matrix()
    if first:
        o1 = sb_prompt(q1, k1b, v1b, ntri)
    else:
        n_new = LANES
        o1 = sb_decode(q1.reshape(b, t, d), _pad_rows(k1b.reshape(b, t, d), n_new),
                       _pad_rows(v1b.reshape(b, t, d), n_new), sk_c, sv_c, ntri, sb_h).reshape(m, d)
    h = out_proj(o1, o1, 1, wts["w_out_sb"], h)
    y = ffn(h, wts["ln_ffn1"], wts["w_gate"], wts["w_up"], wts["w_down"], 1, wts["ln_final"], True)

    return (y.reshape(b, t, d), pool_state[None], k.reshape(1, b, t, fox_h, HEAD_DIM),
            v.reshape(1, b, t, fox_h, HEAD_DIM), log_f.reshape(1, b, t, fox_h),
            k1.reshape(1, b, t, sb_h, HEAD_DIM), v1.reshape(1, b, t, sb_h, HEAD_DIM))


def kernel(x_prompt, x_sample, cache_pool, cache_fox_k, cache_fox_v, cache_fox_logf, cache_sb_k, cache_sb_v, ln_mix, w_in_ab, b_forget, w_pool, pool_scale, w_out_ab, w_qkv_sb, w_out_sb, ln_ffn, w_gate, w_up, w_down, ln_final):
    fox_h = b_forget.shape[-1]
    n_main = w_in_ab.shape[-1] - fox_h
    wts = {
        "ln_mix0": ln_mix[0][None], "ln_mix1": ln_mix[1][None],
        "ln_ffn0": ln_ffn[0][None], "ln_ffn1": ln_ffn[1][None], "ln_final": ln_final[None],
        "w_in": w_in_ab[0, :, :n_main].astype(BF16),
        "w_forget": jnp.pad(w_in_ab[0, :, n_main:], ((0, 0), (0, LANES - fox_h))).astype(BF16),
        "b_forget": jnp.pad(b_forget[0], (0, LANES - fox_h))[None],
        "w_pool": w_pool[0].astype(BF16), "pool_scale": pool_scale[0][None],
        "w_out_ab": w_out_ab[0].astype(BF16), "w_qkv": w_qkv_sb[0].astype(BF16),
        "w_out_sb": w_out_sb[0].astype(BF16),
        "w_gate": w_gate.astype(BF16), "w_up": w_up.astype(BF16), "w_down": w_down.astype(BF16),
    }
    out_p = _run_trunk(x_prompt, 0, None, None, None, None, None, None, wts)
    n_past = cache_fox_k.shape[2]
    bs = x_sample.shape[0]

    def rows_of(cache):
        return cache.reshape(bs, -1, HEAD_DIM)

    out_s = _run_trunk(x_sample, n_past, cache_pool[0], rows_of(cache_fox_k), rows_of(cache_fox_v),
                       cache_fox_logf[0], rows_of(cache_sb_k), rows_of(cache_sb_v), wts)
    return (out_p[0], out_s[0]) + out_p[1:] + out_s[1:]
```

```python
import functools

import numpy as np
import jax
import jax.numpy as jnp
from jax import lax
from jax.experimental import pallas as pl
from jax.experimental.pallas import tpu as pltpu

HEAD_DIM = 128
EPS = 1e-6
POOL_WINDOWS = (2, 4, 8, 16)
POOL_STATE = max(POOL_WINDOWS) - 1
POOL_HALO = 16
LANES = 128
TRI = 256
VMEM_LIMIT = 56 * 1024 * 1024
LOG2E = 1.4426950408889634
QK_SCALE2 = HEAD_DIM ** -0.5 * LOG2E

F32 = jnp.float32
BF16 = jnp.bfloat16


def _cparams(sem, vmem=VMEM_LIMIT):
    return pltpu.CompilerParams(dimension_semantics=sem, vmem_limit_bytes=vmem)


def _resident(shape, index_map):
    return pl.BlockSpec(shape, index_map, pipeline_mode=pl.Buffered(1))


def _dot(a, b):
    return jnp.dot(a, b, preferred_element_type=F32)


def _dot_nt(a, b):
    return lax.dot_general(a, b, (((1,), (1,)), ((), ())), preferred_element_type=F32)


def _rms(x, g):
    ms = jnp.mean(x * x, axis=-1, keepdims=True)
    return x * lax.rsqrt(ms + EPS) * g


def _softplus(z):
    return jnp.maximum(z, 0.0) + jnp.log1p(jnp.exp(-jnp.abs(z)))


def _softplus2(z2):
    return jnp.maximum(z2, 0.0) + jnp.log(1.0 + jnp.exp2(-jnp.abs(z2))) * LOG2E


def _tile_lanes(x, n):
    reps = n // LANES
    return x if reps == 1 else jnp.concatenate([x] * reps, axis=1)


def _head_cols(h):
    return slice(h * HEAD_DIM, (h + 1) * HEAD_DIM)


def _head_rows(ref, h, heads, n):
    return ref[pl.ds(h, n, stride=heads), :]


def _norm_proj_kernel(x_ref, g_ref, w_ref, w2_ref, b_ref, *out_refs, groups):
    xn = _rms(x_ref[...], g_ref[...]).astype(BF16)
    for w_col, width, outs in groups:
        acc = _dot(xn, w2_ref[...] if w_col is None else w_ref[:, w_col:w_col + width])
        for out_idx, out_col, kind in outs:
            if kind == "logsig":
                val = -_softplus(-(acc + b_ref[...]))
            elif kind == "qscaled":
                val = acc * QK_SCALE2
            else:
                val = acc
            o = out_refs[out_idx]
            o[:, out_col:out_col + width] = val.astype(o.dtype)


def norm_proj(x, g, w, w2, bias, groups, out_defs, tm):
    m, d = x.shape
    n = w.shape[1]
    tm = min(tm, m)
    out_shape = [jax.ShapeDtypeStruct((m, c), dt) for c, dt in out_defs]
    out_specs = [pl.BlockSpec((tm, c), lambda i: (i, 0)) for c, _ in out_defs]
    return pl.pallas_call(
        functools.partial(_norm_proj_kernel, groups=groups),
        grid=(m // tm,),
        in_specs=[
            pl.BlockSpec((tm, d), lambda i: (i, 0)),
            _resident((1, d), lambda i: (0, 0)),
            _resident((d, n), lambda i: (0, 0)),
            _resident((d, LANES), lambda i: (0, 0)),
            _resident((1, LANES), lambda i: (0, 0)),
        ],
        out_specs=out_specs,
        out_shape=out_shape,
        compiler_params=_cparams(("parallel",)),
        name="norm_proj",
    )(x, g, w, w2, bias)


def _cumsum_kernel(x_ref, o_ref, carry_ref, *, tb):
    @pl.when(pl.program_id(0) == 0)
    def _():
        carry_ref[...] = jnp.zeros_like(carry_ref)

    x = x_ref[...]
    row = lax.broadcasted_iota(jnp.int32, (tb, tb), 0)
    col = lax.broadcasted_iota(jnp.int32, (tb, tb), 1)
    low = jnp.where(col <= row, 1.0, 0.0).astype(BF16)
    hi = x.astype(BF16)
    r1 = x - hi.astype(F32)
    mid = r1.astype(BF16)
    lo = (r1 - mid.astype(F32)).astype(BF16)
    out = _dot(low, hi) + _dot(low, mid) + _dot(low, lo) + carry_ref[0:1, :]
    o_ref[...] = out
    carry_ref[...] = jnp.broadcast_to(out[tb - 1:tb, :], carry_ref.shape)


def cumsum_rows(x, tb=512):
    t, c = x.shape
    tb = min(tb, t)
    return pl.pallas_call(
        functools.partial(_cumsum_kernel, tb=tb),
        grid=(t // tb,),
        in_specs=[pl.BlockSpec((tb, c), lambda i: (i, 0))],
        out_specs=pl.BlockSpec((tb, c), lambda i: (i, 0)),
        out_shape=jax.ShapeDtypeStruct((t, c), F32),
        scratch_shapes=[pltpu.VMEM((8, c), F32)],
        compiler_params=_cparams(("arbitrary",)),
        name="cumsum_rows",
    )(x)


def _pool_kernel(u_ref, pre_ref, w_ref, sc_ref, o_ref, buf_ref, *, tm, pos0, group):
    mi = pl.program_id(1)

    @pl.when(mi == 0)
    def _():
        buf_ref[0:POOL_HALO, :] = pre_ref[...]

    @pl.when(mi > 0)
    def _():
        buf_ref[0:POOL_HALO, :] = buf_ref[tm:tm + POOL_HALO, :]

    buf_ref[POOL_HALO:POOL_HALO + tm, :] = u_ref[...]
    pos = pos0 + mi * tm + lax.broadcasted_iota(jnp.int32, (tm, 1), 0)
    for gi, win in enumerate(POOL_WINDOWS):
        cs = slice(gi * group, (gi + 1) * group)
        tot = buf_ref[POOL_HALO:POOL_HALO + tm, cs]
        for back in range(1, win):
            tot = tot + buf_ref[POOL_HALO - back:POOL_HALO - back + tm, cs]
        count = jnp.minimum(pos + 1, win).astype(F32)
        diff = tot / count - u_ref[:, cs]
        y = _dot(diff.astype(BF16), w_ref[gi]) * sc_ref[:, cs]
        o_ref[:, cs] = y.astype(o_ref.dtype)


def pool_mixer(u, prefix, w_pool, scale, pos0, tm=512):
    b, t, c = u.shape
    tm = min(tm, t)
    group = c // len(POOL_WINDOWS)
    return pl.pallas_call(
        functools.partial(_pool_kernel, tm=tm, pos0=pos0, group=group),
        grid=(b, t // tm),
        in_specs=[
            pl.BlockSpec((None, tm, c), lambda bi, mi: (bi, mi, 0)),
            pl.BlockSpec((None, POOL_HALO, c), lambda bi, mi: (bi, 0, 0)),
            _resident(w_pool.shape, lambda bi, mi: (0, 0, 0)),
            _resident((1, c), lambda bi, mi: (0, 0)),
        ],
        out_specs=pl.BlockSpec((None, tm, c), lambda bi, mi: (bi, mi, 0)),
        out_shape=jax.ShapeDtypeStruct((b, t, c), BF16),
        scratch_shapes=[pltpu.VMEM((POOL_HALO + tm, c), F32)],
        compiler_params=_cparams(("arbitrary", "arbitrary")),
        name="pool_mixer",
    )(u, prefix, w_pool, scale)


def _fox_update(s2, v, m_prev, l_prev, acc_prev):
    m_new = jnp.maximum(m_prev, jnp.max(s2, axis=1, keepdims=True))
    alpha = jnp.exp2(m_prev - m_new)
    p = jnp.exp2(s2 - _tile_lanes(m_new, s2.shape[1]))
    l_new = alpha * l_prev + jnp.sum(p, axis=1, keepdims=True)
    acc_new = alpha * acc_prev + _dot(p.astype(BF16), v)
    return m_new, l_new, acc_new


FOX_DEAD = -160.0


def _fox_prompt_kernel(ck_end_ref, q_ref, k_ref, v_ref, cq_ref, ck_ref, o_ref, m_ref, l_ref, acc_ref, kmax_ref,
                       *, tq, bs):
    hi = pl.program_id(0)
    qi = pl.program_id(1)
    nq = pl.num_programs(1)
    nsub = tq // bs

    @pl.when(qi == 0)
    def _():
        def norm_body(j, best):
            kb = k_ref[pl.ds(pl.multiple_of(j * tq, tq), tq), :].astype(F32)
            return jnp.maximum(best, jnp.sum(kb * kb, axis=1, keepdims=True))

        best = lax.fori_loop(0, nq, norm_body, jnp.zeros((tq, 1), F32))
        kmax_ref[...] = jnp.broadcast_to(jnp.sqrt(jnp.max(best, axis=0, keepdims=True)), kmax_ref.shape)

    row = lax.broadcasted_iota(jnp.int32, (bs, bs), 0)
    col = lax.broadcasted_iota(jnp.int32, (bs, bs), 1)

    def sub_rows(s):
        return slice(s * bs, (s + 1) * bs)

    def update(s, j, nblk, causal, fresh=False):
        width = nblk * bs
        rows = pl.ds(pl.multiple_of(j * bs, bs), width)
        cq = cq_ref[sub_rows(s), :]
        ck = ck_ref[j] if nblk == 1 else jnp.concatenate([ck_ref[j + i] for i in range(nblk)], axis=1)
        s2 = _dot_nt(q_ref[sub_rows(s), :], k_ref[rows, :]) + _tile_lanes(cq, width) - ck
        if causal:
            tail = jnp.where(col <= row, s2[:, width - bs:], -jnp.inf)
            s2 = tail if nblk == 1 else jnp.concatenate([s2[:, :width - bs], tail], axis=1)
        if fresh:
            state = (jnp.full((bs, LANES), -jnp.inf, F32), jnp.zeros((bs, LANES), F32), jnp.zeros((bs, HEAD_DIM), F32))
        else:
            state = (m_ref[s], l_ref[s], acc_ref[s])
        m_ref[s], l_ref[s], acc_ref[s] = _fox_update(s2, v_ref[rows, :], *state)

    def first_units(at_origin):
        for s in range(nsub):
            if s == 0 and at_origin:
                update(s, 0, 1, True, fresh=True)
            else:
                update(s, qi * nsub + s - 1, 2, True, fresh=True)

    @pl.when(qi == 0)
    def _():
        first_units(True)

    @pl.when(qi > 0)
    def _():
        first_units(False)

    def rest(s):
        g = qi * nsub + s
        qf = q_ref[sub_rows(s), :].astype(F32)
        reach = (jnp.sqrt(jnp.sum(qf * qf, axis=1, keepdims=True)) * kmax_ref[0:1, 0:1]
                 + cq_ref[sub_rows(s), 0:1])

        def margin():
            return jnp.max(reach - m_ref[s][:, 0:1])

        def live(gap, j_last):
            return gap - ck_end_ref[hi, jnp.maximum(j_last, 0)] > FOX_DEAD

        n_rest = jnp.maximum(g - 1, 0)
        gap0 = margin()
        pair = n_rest >= 2

        @pl.when(jnp.logical_and(pair, live(gap0, n_rest - 1)))
        def _():
            update(s, n_rest - 2, 2, False)

        n_left = jnp.where(pair, n_rest - 2, n_rest)

        def cond(state):
            i, alive = state
            return jnp.logical_and(i < n_left // 4, alive)

        def body(state):
            i, _ = state
            update(s, n_left - 4 - 4 * i, 4, False)
            return i + 1, live(margin(), n_left - 5 - 4 * i)

        _, alive = lax.while_loop(cond, body, (0, live(gap0, n_left - 1)))
        rem = n_left % 4

        @pl.when(jnp.logical_and(rem >= 2, alive))
        def _():
            update(s, rem % 2, 2, False)

        @pl.when(jnp.logical_and(rem % 2 == 1, alive))
        def _():
            update(s, 0, 1, False)

        o_ref[sub_rows(s), :] = (acc_ref[s] / l_ref[s]).astype(o_ref.dtype)

    for s in range(nsub):
        rest(s)


def fox_prompt(q, k, v, cq, ck, ck_end, tq, bs):
    t, hd = q.shape
    h = hd // HEAD_DIM
    nq = t // tq
    return pl.pallas_call(
        functools.partial(_fox_prompt_kernel, tq=tq, bs=bs),
        grid=(h, nq),
        in_specs=[
            pl.BlockSpec(memory_space=pltpu.SMEM),
            pl.BlockSpec((tq, HEAD_DIM), lambda hi, qi: (qi, hi)),
            pl.BlockSpec((t, HEAD_DIM), lambda hi, qi: (0, hi)),
            pl.BlockSpec((t, HEAD_DIM), lambda hi, qi: (0, hi)),
            pl.BlockSpec((None, tq, LANES), lambda hi, qi: (hi, qi, 0)),
            pl.BlockSpec((None, t // bs, 1, bs), lambda hi, qi: (hi, 0, 0, 0)),
        ],
        out_specs=pl.BlockSpec((tq, HEAD_DIM), lambda hi, qi: (qi, hi)),
        out_shape=jax.ShapeDtypeStruct((t, hd), BF16),
        scratch_shapes=[pltpu.VMEM((tq // bs, bs, LANES), F32), pltpu.VMEM((tq // bs, bs, LANES), F32),
                        pltpu.VMEM((tq // bs, bs, HEAD_DIM), F32), pltpu.VMEM((8, LANES), F32)],
        compiler_params=_cparams(("arbitrary", "arbitrary")),
        name="fox_prompt",
    )(ck_end, q, k, v, cq, ck)


def _fox_decode_kernel(q_ref, kn_ref, vn_ref, kc_ref, vc_ref, cq_ref, ckc_ref, ckn_ref, o_ref,
                       m_ref, l_ref, acc_ref, *, heads, tk):
    kb = pl.program_id(1)
    tq = q_ref.shape[0]
    n_new = kn_ref.shape[0]

    @pl.when(kb == 0)
    def _():
        row = lax.broadcasted_iota(jnp.int32, (tq, n_new), 0)
        col = lax.broadcasted_iota(jnp.int32, (tq, n_new), 1)
        for h in range(heads):
            hc = _head_cols(h)
            s2 = _dot_nt(q_ref[:, hc], kn_ref[:, hc]) + _tile_lanes(cq_ref[h], n_new) - ckn_ref[h]
            s2 = jnp.where(col <= row, s2, -jnp.inf)
            m = jnp.max(s2, axis=1, keepdims=True)
            p = jnp.exp2(s2 - m)
            m_ref[h] = jnp.broadcast_to(m, (tq, LANES))
            l_ref[h] = jnp.broadcast_to(jnp.sum(p, axis=1, keepdims=True), (tq, LANES))
            acc_ref[h] = _dot(p.astype(BF16), vn_ref[:, hc])

    for h in range(heads):
        k_h = _head_rows(kc_ref, h, heads, tk).astype(BF16)
        v_h = _head_rows(vc_ref, h, heads, tk).astype(BF16)
        s2 = _dot_nt(q_ref[:, _head_cols(h)], k_h) + _tile_lanes(cq_ref[h], tk) - ckc_ref[h]
        m_ref[h], l_ref[h], acc_ref[h] = _fox_update(s2, v_h, m_ref[h], l_ref[h], acc_ref[h])

    @pl.when(kb == pl.num_programs(1) - 1)
    def _():
        for h in range(heads):
            o_ref[:, _head_cols(h)] = (acc_ref[h] / l_ref[h]).astype(o_ref.dtype)


def fox_decode(q, k_new, v_new, k_cache, v_cache, cq, ck_cache, ck_new, heads, tk=1024):
    b, tq, hd = q.shape
    n_new = k_new.shape[1]
    p_len = k_cache.shape[1] // heads
    tk = min(tk, p_len)
    return pl.pallas_call(
        functools.partial(_fox_decode_kernel, heads=heads, tk=tk),
        grid=(b, p_len // tk),
        in_specs=[
            pl.BlockSpec((None, tq, hd), lambda bi, kb: (bi, 0, 0)),
            pl.BlockSpec((None, n_new, hd), lambda bi, kb: (bi, 0, 0)),
            pl.BlockSpec((None, n_new, hd), lambda bi, kb: (bi, 0, 0)),
            pl.BlockSpec((None, tk * heads, HEAD_DIM), lambda bi, kb: (bi, kb, 0)),
            pl.BlockSpec((None, tk * heads, HEAD_DIM), lambda bi, kb: (bi, kb, 0)),
            pl.BlockSpec((None, heads, tq, LANES), lambda bi, kb: (bi, 0, 0, 0)),
            pl.BlockSpec((None, heads, 1, tk), lambda bi, kb: (bi, 0, 0, kb)),
            pl.BlockSpec((None, heads, 1, n_new), lambda bi, kb: (bi, 0, 0, 0)),
        ],
        out_specs=pl.BlockSpec((None, tq, hd), lambda bi, kb: (bi, 0, 0)),
        out_shape=jax.ShapeDtypeStruct((b, tq, hd), BF16),
        scratch_shapes=[pltpu.VMEM((heads, tq, LANES), F32), pltpu.VMEM((heads, tq, LANES), F32),
                        pltpu.VMEM((heads, tq, HEAD_DIM), F32)],
        compiler_params=_cparams(("parallel", "arbitrary")),
        name="fox_decode",
    )(q, k_new, v_new, k_cache, v_cache, cq, ck_cache, ck_new)


SB_DEAD = -160.0


def _suffix_sums(sp, ntri, carry):
    width = ntri.shape[0]
    n = sp.shape[1]
    parts = []
    for c0 in range(n - width, -1, -width):
        piece = sp[:, c0:c0 + width]
        hi = piece.astype(BF16)
        lo = (piece - hi.astype(F32)).astype(BF16)
        c = _dot(hi, ntri) + _dot(lo, ntri) + carry
        carry = c[:, 0:1]
        parts.append(c)
    parts.reverse()
    return (parts[0] if len(parts) == 1 else jnp.concatenate(parts, axis=1)), carry


def _mask_tail(x, mask):
    if mask is None:
        return x
    head = x.shape[1] - mask.shape[1]
    tail = jnp.where(mask, x[:, head:], 0.0)
    return tail if head == 0 else jnp.concatenate([x[:, :head], tail], axis=1)


def _sb_weights(z2, ntri, carry, mask):
    sp = _mask_tail(_softplus2(z2), mask)
    c, new_carry = _suffix_sums(sp, ntri, carry)
    a = _mask_tail(jnp.exp2(z2 + c), mask)
    return a.astype(BF16), new_carry


def _sb_prompt_kernel(q_ref, k_ref, v_ref, ntri_ref, o_ref, r_ref, acc_ref, *, tq, bs):
    qi = pl.program_id(1)
    nsub = tq // bs
    ntri = ntri_ref[...]
    row = lax.broadcasted_iota(jnp.int32, (bs, bs), 0)
    col = lax.broadcasted_iota(jnp.int32, (bs, bs), 1)

    def sub_rows(s):
        return slice(s * bs, (s + 1) * bs)

    def block(s, j, nblk, carry, mask):
        rows = pl.ds(pl.multiple_of(j * bs, bs), nblk * bs)
        a, carry = _sb_weights(_dot_nt(q_ref[sub_rows(s), :], k_ref[rows, :]), ntri, carry, mask)
        return _dot(a, v_ref[rows, :]), carry

    def first_units(at_origin):
        for s in range(nsub):
            if s == 0 and at_origin:
                out, carry = block(s, 0, 1, jnp.zeros((bs, 1), F32), col < row)
            else:
                out, carry = block(s, qi * nsub + s - 1, 2, jnp.zeros((bs, 1), F32), col < row)
            acc_ref[s] = out
            r_ref[s] = jnp.broadcast_to(carry, (bs, LANES))

    @pl.when(qi == 0)
    def _():
        first_units(True)

    @pl.when(qi > 0)
    def _():
        first_units(False)

    def live(s):
        return jnp.max(r_ref[s]) > SB_DEAD

    alive_after_first = [live(s) for s in range(nsub)]

    def rest(s):
        g = qi * nsub + s

        def step(j, nblk):
            out, carry = block(s, j, nblk, r_ref[s][:, 0:1], None)
            acc_ref[s] += out
            r_ref[s] = jnp.broadcast_to(carry, (bs, LANES))

        n_rest = jnp.maximum(g - 1, 0)

        def cond(state):
            i, alive = state
            return jnp.logical_and(i < n_rest // 4, alive)

        def body(state):
            i, _ = state
            step(n_rest - 4 - 4 * i, 4)
            return i + 1, live(s)

        _, alive = lax.while_loop(cond, body, (0, alive_after_first[s]))

        @pl.when(jnp.logical_and(n_rest % 4 >= 2, alive))
        def _():
            step(n_rest % 2, 2)

        @pl.when(jnp.logical_and(n_rest % 2 == 1, alive))
        def _():
            step(0, 1)

        o_ref[sub_rows(s), :] = acc_ref[s].astype(o_ref.dtype)

    for s in range(nsub):
        rest(s)


def sb_prompt(q, k, v, ntri, tq=1024, bs=256):
    t, hd = q.shape
    h = hd // HEAD_DIM
    tq = min(tq, t)
    bs = min(bs, tq)
    return pl.pallas_call(
        functools.partial(_sb_prompt_kernel, tq=tq, bs=bs),
        grid=(h, t // tq),
        in_specs=[
            pl.BlockSpec((tq, HEAD_DIM), lambda hi, qi: (qi, hi)),
            pl.BlockSpec((t, HEAD_DIM), lambda hi, qi: (0, hi)),
            pl.BlockSpec((t, HEAD_DIM), lambda hi, qi: (0, hi)),
            _resident(ntri.shape, lambda hi, qi: (0, 0)),
        ],
        out_specs=pl.BlockSpec((tq, HEAD_DIM), lambda hi, qi: (qi, hi)),
        out_shape=jax.ShapeDtypeStruct((t, hd), BF16),
        scratch_shapes=[pltpu.VMEM((tq // bs, bs, LANES), F32), pltpu.VMEM((tq // bs, bs, HEAD_DIM), F32)],
        compiler_params=_cparams(("parallel", "arbitrary")),
        name="sb_prompt",
    )(q, k, v, ntri)


def _sb_decode_kernel(q_ref, kn_ref, vn_ref, kc_hbm, vc_hbm, ntri_ref, o_ref, kbuf, vbuf, sem, r_ref, acc_ref,
                      *, heads, tk):
    b = pl.program_id(0)
    tq = q_ref.shape[0]
    n_new = kn_ref.shape[0]
    blk = tk * heads
    nkb = kc_hbm.shape[1] // blk

    def fetch(j, slot):
        rows = pl.ds(pl.multiple_of(j * blk, blk), blk)
        return (pltpu.make_async_copy(kc_hbm.at[b, rows, :], kbuf.at[slot], sem.at[0, slot]),
                pltpu.make_async_copy(vc_hbm.at[b, rows, :], vbuf.at[slot], sem.at[1, slot]))

    for copy in fetch(nkb - 1, 0):
        copy.start()

    def head_rows(h):
        return slice(h * tq, (h + 1) * tq)

    def stacked_scores(keys_of):
        return jnp.concatenate([_dot_nt(q_ref[:, _head_cols(h)], keys_of(h)) for h in range(heads)], axis=0)

    def live():
        return jnp.max(r_ref[...]) > SB_DEAD

    z2 = stacked_scores(lambda h: kn_ref[:, _head_cols(h)])
    row = lax.rem(lax.broadcasted_iota(jnp.int32, (heads * tq, n_new), 0), tq)
    col = lax.broadcasted_iota(jnp.int32, (heads * tq, n_new), 1)
    a, carry = _sb_weights(z2, ntri_ref[0:n_new, 0:n_new], jnp.zeros((heads * tq, 1), F32), col < row)
    for h in range(heads):
        acc_ref[head_rows(h), :] = _dot(a[head_rows(h), :], vn_ref[:, _head_cols(h)])
    r_ref[...] = jnp.broadcast_to(carry, r_ref.shape)

    def body(state):
        i, _ = state
        j = nkb - 1 - i
        slot = lax.rem(i, 2)
        for copy in fetch(j, slot):
            copy.wait()

        @pl.when(j > 0)
        def _():
            for copy in fetch(j - 1, 1 - slot):
                copy.start()

        z2 = stacked_scores(lambda h: _head_rows(kbuf.at[slot], h, heads, tk).astype(BF16))
        a, carry = _sb_weights(z2, ntri_ref[...], r_ref[:, 0:1], None)
        for h in range(heads):
            acc_ref[head_rows(h), :] += _dot(a[head_rows(h), :], _head_rows(vbuf.at[slot], h, heads, tk).astype(BF16))
        r_ref[...] = jnp.broadcast_to(carry, r_ref.shape)
        return i + 1, live()

    n_done, _ = lax.while_loop(lambda st: jnp.logical_and(st[0] < nkb, st[1]), body, (0, live()))

    @pl.when(n_done < nkb)
    def _():
        for copy in fetch(nkb - 1 - n_done, lax.rem(n_done, 2)):
            copy.wait()

    for h in range(heads):
        o_ref[:, _head_cols(h)] = acc_ref[head_rows(h), :].astype(o_ref.dtype)


def sb_decode(q, k_new, v_new, k_cache, v_cache, ntri, heads, tk=512):
    b, tq, hd = q.shape
    n_new = k_new.shape[1]
    p_len = k_cache.shape[1] // heads
    tk = min(tk, p_len)
    return pl.pallas_call(
        functools.partial(_sb_decode_kernel, heads=heads, tk=tk),
        grid=(b,),
        in_specs=[
            pl.BlockSpec((None, tq, hd), lambda bi: (bi, 0, 0)),
            pl.BlockSpec((None, n_new, hd), lambda bi: (bi, 0, 0)),
            pl.BlockSpec((None, n_new, hd), lambda bi: (bi, 0, 0)),
            pl.BlockSpec(memory_space=pl.ANY),
            pl.BlockSpec(memory_space=pl.ANY),
            _resident(ntri.shape, lambda bi: (0, 0)),
        ],
        out_specs=pl.BlockSpec((None, tq, hd), lambda bi: (bi, 0, 0)),
        out_shape=jax.ShapeDtypeStruct((b, tq, hd), BF16),
        scratch_shapes=[pltpu.VMEM((2, tk * heads, HEAD_DIM), F32), pltpu.VMEM((2, tk * heads, HEAD_DIM), F32),
                        pltpu.SemaphoreType.DMA((2, 2)),
                        pltpu.VMEM((heads * tq, LANES), F32), pltpu.VMEM((heads * tq, HEAD_DIM), F32)],
        compiler_params=_cparams(("arbitrary",)),
        name="sb_decode",
    )(q, k_new, v_new, k_cache, v_cache, ntri)


def _out_proj_kernel(xa_ref, xb_ref, w_ref, h_ref, o_ref):
    ka = xa_ref.shape[1]
    o_ref[...] = h_ref[...] + _dot(xa_ref[...], w_ref[:ka, :]) + _dot(xb_ref[...], w_ref[ka:, :])


def out_proj(xa, xb, col_b, w, h, tm=512):
    m, d = h.shape
    ka = w.shape[0] // 2
    tm = min(tm, m)
    return pl.pallas_call(
        _out_proj_kernel,
        grid=(m // tm,),
        in_specs=[
            pl.BlockSpec((tm, ka), lambda i: (i, 0)),
            pl.BlockSpec((tm, ka), lambda i: (i, col_b)),
            _resident(w.shape, lambda i: (0, 0)),
            pl.BlockSpec((tm, d), lambda i: (i, 0)),
        ],
        out_specs=pl.BlockSpec((tm, d), lambda i: (i, 0)),
        out_shape=jax.ShapeDtypeStruct((m, d), F32),
        compiler_params=_cparams(("parallel",)),
        name="out_proj",
    )(xa, xb, w, h)


def _ffn_kernel(h_ref, g_ref, wg_ref, wu_ref, wd_ref, gf_ref, o_ref, xn_ref, *, final_norm):
    fi = pl.program_id(1)

    @pl.when(fi == 0)
    def _():
        xn_ref[...] = _rms(h_ref[...], g_ref[...]).astype(BF16)
        o_ref[...] = jnp.zeros_like(o_ref)

    xn = xn_ref[...]
    gate = _dot(xn, wg_ref[...])
    up = _dot(xn, wu_ref[...])
    mid = (gate / (1.0 + jnp.exp(-gate)) * up).astype(BF16)
    width = wd_ref.shape[0]
    for c0 in range(0, o_ref.shape[1], width):
        o_ref[:, c0:c0 + width] += _dot(mid, wd_ref[:, c0:c0 + width])

    @pl.when(fi == pl.num_programs(1) - 1)
    def _():
        hn = h_ref[...] + o_ref[...]
        o_ref[...] = _rms(hn, gf_ref[...]) if final_norm else hn


def ffn(h, g, w_gate, w_up, w_down, layer, g_final, final_norm, tm=1024, tf=512):
    m, d = h.shape
    f = w_gate.shape[2]
    tm = min(tm, m)
    return pl.pallas_call(
        functools.partial(_ffn_kernel, final_norm=final_norm),
        grid=(m // tm, f // tf),
        in_specs=[
            pl.BlockSpec((tm, d), lambda i, j: (i, 0), pipeline_mode=pl.Buffered(1)),
            _resident((1, d), lambda i, j: (0, 0)),
            pl.BlockSpec((None, d, tf), lambda i, j: (layer, 0, j)),
            pl.BlockSpec((None, d, tf), lambda i, j: (layer, 0, j)),
            pl.BlockSpec((None, tf, d), lambda i, j: (layer, j, 0)),
            _resident((1, d), lambda i, j: (0, 0)),
        ],
        out_specs=pl.BlockSpec((tm, d), lambda i, j: (i, 0)),
        out_shape=jax.ShapeDtypeStruct((m, d), F32),
        scratch_shapes=[pltpu.VMEM((tm, d), BF16)],
        compiler_params=_cparams(("parallel", "arbitrary")),
        name="ffn",
    )(h, g, w_gate, w_up, w_down, g_final)


def _pad_rows(x, rows):
    return jnp.pad(x, ((0, 0), (0, rows - x.shape[1]), (0, 0)))


def _neg_tri_matrix():
    idx = np.arange(TRI)
    return jnp.asarray(-(idx[:, None] >= idx[None, :]).astype(np.float32), dtype=BF16)


def _layer0_groups(pool_w, fox_w):
    outs = [("f32",), ("qscaled",), ("f32", "bf16"), ("f32", "bf16")]
    groups, out_defs, col = [], [], 0
    for kinds in outs:
        width = pool_w if col == 0 else fox_w
        o = []
        for kind in kinds:
            o.append((len(out_defs), 0, kind))
            out_defs.append((width, F32 if kind == "f32" else BF16))
        groups.append((col, width, tuple(o)))
        col += width
    groups.append((None, LANES, ((len(out_defs), 0, "logsig"),)))
    out_defs.append((LANES, F32))
    return tuple(groups), out_defs


def _layer1_groups(d):
    half = d // 2
    groups = []
    for part, outs in enumerate([(0,), (1, 2), (3, 4)]):
        for c in range(2):
            kinds = ("qscaled",) if part == 0 else ("f32", "bf16")
            groups.append((part * d + c * half, half,
                           tuple((oi, c * half, kind) for oi, kind in zip(outs, kinds))))
    out_defs = [(d, BF16), (d, F32), (d, BF16), (d, F32), (d, BF16)]
    return tuple(groups), out_defs


def _run_trunk(x, n_past, pool_c, fk_c, fv_c, flf_c, sk_c, sv_c, wts):
    b, t, d = x.shape
    m = b * t
    h = x.reshape(m, d)
    pool_w = wts["w_pool"].shape[0] * wts["w_pool"].shape[1]
    fox_w = d - pool_w
    fox_h = fox_w // HEAD_DIM
    sb_h = d // HEAD_DIM
    first = fk_c is None
    zero_bias = jnp.zeros((1, LANES), F32)

    groups, out_defs = _layer0_groups(pool_w, fox_w)
    u, q, k, kb, v, vb, lf = norm_proj(h, wts["ln_mix0"], wts["w_in"], wts["w_forget"], wts["b_forget"], groups,
                                        out_defs, tm=256)
    log_f = lf[:, :fox_h]
    if first:
        prefix = jnp.zeros((b, POOL_HALO, pool_w), F32)
    else:
        prefix = jnp.pad(pool_c, ((0, 0), (POOL_HALO - POOL_STATE, 0), (0, 0)))
    y_pool = pool_mixer(u.reshape(b, t, pool_w), prefix, wts["w_pool"], wts["pool_scale"], n_past)
    if t >= POOL_STATE:
        pool_state = u.reshape(b, t, pool_w)[:, t - POOL_STATE:]
    else:
        pool_state = jnp.concatenate([prefix[:, 1:], u.reshape(b, t, pool_w)], axis=1)[:, -POOL_STATE:]

    if first:
        tq = bs = min(512, t)
        cum2 = cumsum_rows(lf)[:, :fox_h].T * LOG2E
        cq = jnp.broadcast_to(cum2[:, :, None], (fox_h, t, LANES))
        ck = cum2.reshape(fox_h, t // bs, 1, bs)
        o = fox_prompt(q, kb, vb, cq, ck, ck[:, :, 0, bs - 1], tq, bs)
    else:
        n_new = LANES
        lf_new = log_f.reshape(b, t, fox_h).transpose(1, 0, 2).reshape(t, b * fox_h)
        lf_all = jnp.concatenate([flf_c.transpose(1, 0, 2).reshape(n_past, b * fox_h), lf_new], axis=0)
        rows = n_past + t
        rows_pad = -(-rows // 512) * 512
        lf_all = jnp.pad(lf_all, ((0, rows_pad - rows), (0, LANES - b * fox_h)))
        cum2 = cumsum_rows(lf_all)[:rows, :b * fox_h].reshape(rows, b, fox_h).transpose(1, 2, 0) * LOG2E
        ck_cache = cum2[:, :, None, :n_past]
        cum_new = cum2[:, :, n_past:]
        ck_new = jnp.pad(cum_new, ((0, 0), (0, 0), (0, n_new - t)))[:, :, None, :]
        cq = jnp.broadcast_to(cum_new[:, :, :, None], (b, fox_h, t, LANES))
        o = fox_decode(q.reshape(b, t, fox_w), _pad_rows(kb.reshape(b, t, fox_w), n_new),
                       _pad_rows(vb.reshape(b, t, fox_w), n_new), fk_c, fv_c, cq, ck_cache, ck_new, fox_h)
        o = o.reshape(m, fox_w)
    h = out_proj(y_pool.reshape(m, pool_w), o, 0, wts["w_out_ab"], h)
    h = ffn(h, wts["ln_ffn0"], wts["w_gate"], wts["w_up"], wts["w_down"], 0, wts["ln_final"], False)

    groups1, out_defs1 = _layer1_groups(d)
    q1, k1, k1b, v1, v1b = norm_proj(h, wts["ln_mix1"], wts["w_qkv"], wts["w_forget"], zero_bias, groups1, out_defs1,
                                     tm=256)
    ntri = _neg_tri_matrix()
    if first:
        o1 = sb_prompt(q1, k1b, v1b, ntri)
    else:
        n_new = LANES
        o1 = sb_decode(q1.reshape(b, t, d), _pad_rows(k1b.reshape(b, t, d), n_new),
                       _pad_rows(v1b.reshape(b, t, d), n_new), sk_c, sv_c, ntri, sb_h).reshape(m, d)
    h = out_proj(o1, o1, 1, wts["w_out_sb"], h)
    y = ffn(h, wts["ln_ffn1"], wts["w_gate"], wts["w_up"], wts["w_down"], 1, wts["ln_final"], True)

    return (y.reshape(b, t, d), pool_state[None], k.reshape(1, b, t, fox_h, HEAD_DIM),
            v.reshape(1, b, t, fox_h, HEAD_DIM), log_f.reshape(1, b, t, fox_h),
            k1.reshape(1, b, t, sb_h, HEAD_DIM), v1.reshape(1, b, t, sb_h, HEAD_DIM))


def kernel(x_prompt, x_sample, cache_pool, cache_fox_k, cache_fox_v, cache_fox_logf, cache_sb_k, cache_sb_v, ln_mix, w_in_ab, b_forget, w_pool, pool_scale, w_out_ab, w_qkv_sb, w_out_sb, ln_ffn, w_gate, w_up, w_down, ln_final):
    fox_h = b_forget.shape[-1]
    n_main = w_in_ab.shape[-1] - fox_h
    wts = {
        "ln_mix0": ln_mix[0][None], "ln_mix1": ln_mix[1][None],
        "ln_ffn0": ln_ffn[0][None], "ln_ffn1": ln_ffn[1][None], "ln_final": ln_final[None],
        "w_in": w_in_ab[0, :, :n_main].astype(BF16),
        "w_forget": jnp.pad(w_in_ab[0, :, n_main:], ((0, 0), (0, LANES - fox_h))).astype(BF16),
        "b_forget": jnp.pad(b_forget[0], (0, LANES - fox_h))[None],
        "w_pool": w_pool[0].astype(BF16), "pool_scale": pool_scale[0][None],
        "w_out_ab": w_out_ab[0].astype(BF16), "w_qkv": w_qkv_sb[0].astype(BF16),
        "w_out_sb": w_out_sb[0].astype(BF16),
        "w_gate": w_gate.astype(BF16), "w_up": w_up.astype(BF16), "w_down": w_down.astype(BF16),
    }
    out_p = _run_trunk(x_prompt, 0, None, None, None, None, None, None, wts)
    n_past = cache_fox_k.shape[2]
    bs = x_sample.shape[0]

    def rows_of(cache):
        return cache.reshape(bs, -1, HEAD_DIM)

    out_s = _run_trunk(x_sample, n_past, cache_pool[0], rows_of(cache_fox_k), rows_of(cache_fox_v),
                       cache_fox_logf[0], rows_of(cache_sb_k), rows_of(cache_sb_v), wts)
    return (out_p[0], out_s[0]) + out_p[1:] + out_s[1:]
```

```python
import functools

import numpy as np
import jax
import jax.numpy as jnp
from jax import lax
from jax.experimental import pallas as pl
from jax.experimental.pallas import tpu as pltpu

HEAD_DIM = 128
EPS = 1e-6
POOL_WINDOWS = (2, 4, 8, 16)
POOL_STATE = max(POOL_WINDOWS) - 1
POOL_HALO = 16
LANES = 128
TRI = 256
VMEM_LIMIT = 56 * 1024 * 1024
LOG2E = 1.4426950408889634
QK_SCALE2 = HEAD_DIM ** -0.5 * LOG2E

F32 = jnp.float32
BF16 = jnp.bfloat16


def _cparams(sem, vmem=VMEM_LIMIT):
    return pltpu.CompilerParams(dimension_semantics=sem, vmem_limit_bytes=vmem)


def _resident(shape, index_map):
    return pl.BlockSpec(shape, index_map, pipeline_mode=pl.Buffered(1))


def _dot(a, b):
    return jnp.dot(a, b, preferred_element_type=F32)


def _dot_nt(a, b):
    return lax.dot_general(a, b, (((1,), (1,)), ((), ())), preferred_element_type=F32)


def _rms(x, g):
    ms = jnp.mean(x * x, axis=-1, keepdims=True)
    return x * lax.rsqrt(ms + EPS) * g


def _softplus(z):
    return jnp.maximum(z, 0.0) + jnp.log1p(jnp.exp(-jnp.abs(z)))


def _softplus2(z2):
    return jnp.maximum(z2, 0.0) + jnp.log(1.0 + jnp.exp2(-jnp.abs(z2))) * LOG2E


def _tile_lanes(x, n):
    reps = n // LANES
    return x if reps == 1 else jnp.concatenate([x] * reps, axis=1)


def _head_cols(h):
    return slice(h * HEAD_DIM, (h + 1) * HEAD_DIM)


def _head_rows(ref, h, heads, n):
    return ref[pl.ds(h, n, stride=heads), :]


def _norm_proj_kernel(x_ref, g_ref, w_ref, w2_ref, b_ref, *out_refs, groups):
    xn = _rms(x_ref[...], g_ref[...]).astype(BF16)
    for w_col, width, outs in groups:
        acc = _dot(xn, w2_ref[...] if w_col is None else w_ref[:, w_col:w_col + width])
        for out_idx, out_col, kind in outs:
            if kind == "logsig":
                val = -_softplus(-(acc + b_ref[...]))
            elif kind == "qscaled":
                val = acc * QK_SCALE2
            else:
                val = acc
            o = out_refs[out_idx]
            o[:, out_col:out_col + width] = val.astype(o.dtype)


def norm_proj(x, g, w, w2, bias, groups, out_defs, tm):
    m, d = x.shape
    n = w.shape[1]
    tm = min(tm, m)
    out_shape = [jax.ShapeDtypeStruct((m, c), dt) for c, dt in out_defs]
    out_specs = [pl.BlockSpec((tm, c), lambda i: (i, 0)) for c, _ in out_defs]
    return pl.pallas_call(
        functools.partial(_norm_proj_kernel, groups=groups),
        grid=(m // tm,),
        in_specs=[
            pl.BlockSpec((tm, d), lambda i: (i, 0)),
            _resident((1, d), lambda i: (0, 0)),
            _resident((d, n), lambda i: (0, 0)),
            _resident((d, LANES), lambda i: (0, 0)),
            _resident((1, LANES), lambda i: (0, 0)),
        ],
        out_specs=out_specs,
        out_shape=out_shape,
        compiler_params=_cparams(("parallel",)),
        name="norm_proj",
    )(x, g, w, w2, bias)


def _cumsum_kernel(x_ref, o_ref, carry_ref, *, tb):
    @pl.when(pl.program_id(0) == 0)
    def _():
        carry_ref[...] = jnp.zeros_like(carry_ref)

    x = x_ref[...]
    row = lax.broadcasted_iota(jnp.int32, (tb, tb), 0)
    col = lax.broadcasted_iota(jnp.int32, (tb, tb), 1)
    low = jnp.where(col <= row, 1.0, 0.0).astype(BF16)
    hi = x.astype(BF16)
    r1 = x - hi.astype(F32)
    mid = r1.astype(BF16)
    lo = (r1 - mid.astype(F32)).astype(BF16)
    out = _dot(low, hi) + _dot(low, mid) + _dot(low, lo) + carry_ref[0:1, :]
    o_ref[...] = out
    carry_ref[...] = jnp.broadcast_to(out[tb - 1:tb, :], carry_ref.shape)


def cumsum_rows(x, tb=512):
    t, c = x.shape
    tb = min(tb, t)
    return pl.pallas_call(
        functools.partial(_cumsum_kernel, tb=tb),
        grid=(t // tb,),
        in_specs=[pl.BlockSpec((tb, c), lambda i: (i, 0))],
        out_specs=pl.BlockSpec((tb, c), lambda i: (i, 0)),
        out_shape=jax.ShapeDtypeStruct((t, c), F32),
        scratch_shapes=[pltpu.VMEM((8, c), F32)],
        compiler_params=_cparams(("arbitrary",)),
        name="cumsum_rows",
    )(x)


def _pool_kernel(u_ref, pre_ref, w_ref, sc_ref, o_ref, buf_ref, *, tm, pos0, group):
    mi = pl.program_id(1)

    @pl.when(mi == 0)
    def _():
        buf_ref[0:POOL_HALO, :] = pre_ref[...]

    @pl.when(mi > 0)
    def _():
        buf_ref[0:POOL_HALO, :] = buf_ref[tm:tm + POOL_HALO, :]

    buf_ref[POOL_HALO:POOL_HALO + tm, :] = u_ref[...]
    pos = pos0 + mi * tm + lax.broadcasted_iota(jnp.int32, (tm, 1), 0)
    for gi, win in enumerate(POOL_WINDOWS):
        cs = slice(gi * group, (gi + 1) * group)
        tot = buf_ref[POOL_HALO:POOL_HALO + tm, cs]
        for back in range(1, win):
            tot = tot + buf_ref[POOL_HALO - back:POOL_HALO - back + tm, cs]
        count = jnp.minimum(pos + 1, win).astype(F32)
        diff = tot / count - u_ref[:, cs]
        y = _dot(diff.astype(BF16), w_ref[gi]) * sc_ref[:, cs]
        o_ref[:, cs] = y.astype(o_ref.dtype)


def pool_mixer(u, prefix, w_pool, scale, pos0, tm=512):
    b, t, c = u.shape
    tm = min(tm, t)
    group = c // len(POOL_WINDOWS)
    return pl.pallas_call(
        functools.partial(_pool_kernel, tm=tm, pos0=pos0, group=group),
        grid=(b, t // tm),
        in_specs=[
            pl.BlockSpec((None, tm, c), lambda bi, mi: (bi, mi, 0)),
            pl.BlockSpec((None, POOL_HALO, c), lambda bi, mi: (bi, 0, 0)),
            _resident(w_pool.shape, lambda bi, mi: (0, 0, 0)),
            _resident((1, c), lambda bi, mi: (0, 0)),
        ],
        out_specs=pl.BlockSpec((None, tm, c), lambda bi, mi: (bi, mi, 0)),
        out_shape=jax.ShapeDtypeStruct((b, t, c), BF16),
        scratch_shapes=[pltpu.VMEM((POOL_HALO + tm, c), F32)],
        compiler_params=_cparams(("arbitrary", "arbitrary")),
        name="pool_mixer",
    )(u, prefix, w_pool, scale)


def _fox_update(s2, v, m_prev, l_prev, acc_prev):
    m_new = jnp.maximum(m_prev, jnp.max(s2, axis=1, keepdims=True))
    alpha = jnp.exp2(m_prev - m_new)
    p = jnp.exp2(s2 - _tile_lanes(m_new, s2.shape[1]))
    l_new = alpha * l_prev + jnp.sum(p, axis=1, keepdims=True)
    acc_new = alpha * acc_prev + _dot(p.astype(BF16), v)
    return m_new, l_new, acc_new


FOX_DEAD = -160.0


def _fox_prompt_kernel(ck_end_ref, q_ref, k_ref, v_ref, ck_ref, o_ref, m_ref, l_ref, acc_ref, cq_ref, kmax_ref,
                       *, tq, bs):
    hi = pl.program_id(0)
    qi = pl.program_id(1)
    nq = pl.num_programs(1)
    nsub = tq // bs

    @pl.when(qi == 0)
    def _():
        def norm_body(j, best):
            kb = k_ref[pl.ds(pl.multiple_of(j * tq, tq), tq), :].astype(F32)
            return jnp.maximum(best, jnp.sum(kb * kb, axis=1, keepdims=True))

        best = lax.fori_loop(0, nq, norm_body, jnp.zeros((tq, 1), F32))
        kmax_ref[...] = jnp.broadcast_to(jnp.sqrt(jnp.max(best, axis=0, keepdims=True)), kmax_ref.shape)

    row = lax.broadcasted_iota(jnp.int32, (bs, bs), 0)
    col = lax.broadcasted_iota(jnp.int32, (bs, bs), 1)

    def sub_rows(s):
        return slice(s * bs, (s + 1) * bs)

    for s in range(nsub):
        own = jnp.where(row == col, jnp.broadcast_to(ck_ref[qi * nsub + s], (bs, bs)), 0.0)
        cq_ref[sub_rows(s), :] = jnp.broadcast_to(jnp.sum(own, axis=1, keepdims=True), (bs, LANES))

    def update(s, j, nblk, causal, fresh=False):
        width = nblk * bs
        rows = pl.ds(pl.multiple_of(j * bs, bs), width)
        cq = cq_ref[sub_rows(s), :]
        ck = ck_ref[j] if nblk == 1 else jnp.concatenate([ck_ref[j + i] for i in range(nblk)], axis=1)
        s2 = _dot_nt(q_ref[sub_rows(s), :], k_ref[rows, :]) + _tile_lanes(cq, width) - ck
        if causal:
            tail = jnp.where(col <= row, s2[:, width - bs:], -jnp.inf)
            s2 = tail if nblk == 1 else jnp.concatenate([s2[:, :width - bs], tail], axis=1)
        if fresh:
            state = (jnp.full((bs, LANES), -jnp.inf, F32), jnp.zeros((bs, LANES), F32), jnp.zeros((bs, HEAD_DIM), F32))
        else:
            state = (m_ref[s], l_ref[s], acc_ref[s])
        m_ref[s], l_ref[s], acc_ref[s] = _fox_update(s2, v_ref[rows, :], *state)

    def first_units(at_origin):
        for s in range(nsub):
            if s == 0 and at_origin:
                update(s, 0, 1, True, fresh=True)
            else:
                update(s, qi * nsub + s - 1, 2, True, fresh=True)

    @pl.when(qi == 0)
    def _():
        first_units(True)

    @pl.when(qi > 0)
    def _():
        first_units(False)

    def rest(s):
        g = qi * nsub + s
        qf = q_ref[sub_rows(s), :].astype(F32)
        reach = (jnp.sqrt(jnp.sum(qf * qf, axis=1, keepdims=True)) * kmax_ref[0:1, 0:1]
                 + cq_ref[sub_rows(s), 0:1])

        def margin():
            return jnp.max(reach - m_ref[s][:, 0:1])

        def live(gap, j_last):
            return gap - ck_end_ref[hi, jnp.maximum(j_last, 0)] > FOX_DEAD

        n_rest = jnp.maximum(g - 1, 0)
        gap0 = margin()
        pair = n_rest >= 2

        @pl.when(jnp.logical_and(pair, live(gap0, n_rest - 1)))
        def _():
            update(s, n_rest - 2, 2, False)

        n_left = jnp.where(pair, n_rest - 2, n_rest)

        def cond(state):
            i, alive = state
            return jnp.logical_and(i < n_left // 4, alive)

        def body(state):
            i, _ = state
            update(s, n_left - 4 - 4 * i, 4, False)
            return i + 1, live(margin(), n_left - 5 - 4 * i)

        _, alive = lax.while_loop(cond, body, (0, live(gap0, n_left - 1)))
        rem = n_left % 4

        @pl.when(jnp.logical_and(rem >= 2, alive))
        def _():
            update(s, rem % 2, 2, False)

        @pl.when(jnp.logical_and(rem % 2 == 1, alive))
        def _():
            update(s, 0, 1, False)

        o_ref[sub_rows(s), :] = (acc_ref[s] / l_ref[s]).astype(o_ref.dtype)

    for s in range(nsub):
        rest(s)


def fox_prompt(q, k, v, ck, ck_end, tq, bs):
    t, hd = q.shape
    h = hd // HEAD_DIM
    nq = t // tq
    return pl.pallas_call(
        functools.partial(_fox_prompt_kernel, tq=tq, bs=bs),
        grid=(h, nq),
        in_specs=[
            pl.BlockSpec(memory_space=pltpu.SMEM),
            pl.BlockSpec((tq, HEAD_DIM), lambda hi, qi: (qi, hi)),
            pl.BlockSpec((t, HEAD_DIM), lambda hi, qi: (0, hi)),
            pl.BlockSpec((t, HEAD_DIM), lambda hi, qi: (0, hi)),
            pl.BlockSpec((None, t // bs, 1, bs), lambda hi, qi: (hi, 0, 0, 0)),
        ],
        out_specs=pl.BlockSpec((tq, HEAD_DIM), lambda hi, qi: (qi, hi)),
        out_shape=jax.ShapeDtypeStruct((t, hd), BF16),
        scratch_shapes=[pltpu.VMEM((tq // bs, bs, LANES), F32), pltpu.VMEM((tq // bs, bs, LANES), F32),
                        pltpu.VMEM((tq // bs, bs, HEAD_DIM), F32), pltpu.VMEM((tq, LANES), F32),
                        pltpu.VMEM((8, LANES), F32)],
        compiler_params=_cparams(("arbitrary", "arbitrary")),
        name="fox_prompt",
    )(ck_end, q, k, v, ck)


def _fox_decode_kernel(q_ref, kn_ref, vn_ref, kc_ref, vc_ref, cq_ref, ckc_ref, ckn_ref, o_ref,
                       m_ref, l_ref, acc_ref, *, heads, tk):
    kb = pl.program_id(1)
    tq = q_ref.shape[0]
    n_new = kn_ref.shape[0]

    @pl.when(kb == 0)
    def _():
        row = lax.broadcasted_iota(jnp.int32, (tq, n_new), 0)
        col = lax.broadcasted_iota(jnp.int32, (tq, n_new), 1)
        for h in range(heads):
            hc = _head_cols(h)
            s2 = _dot_nt(q_ref[:, hc], kn_ref[:, hc]) + _tile_lanes(cq_ref[h], n_new) - ckn_ref[h]
            s2 = jnp.where(col <= row, s2, -jnp.inf)
            m = jnp.max(s2, axis=1, keepdims=True)
            p = jnp.exp2(s2 - m)
            m_ref[h] = jnp.broadcast_to(m, (tq, LANES))
            l_ref[h] = jnp.broadcast_to(jnp.sum(p, axis=1, keepdims=True), (tq, LANES))
            acc_ref[h] = _dot(p.astype(BF16), vn_ref[:, hc])

    for h in range(heads):
        k_h = _head_rows(kc_ref, h, heads, tk).astype(BF16)
        v_h = _head_rows(vc_ref, h, heads, tk).astype(BF16)
        s2 = _dot_nt(q_ref[:, _head_cols(h)], k_h) + _tile_lanes(cq_ref[h], tk) - ckc_ref[h]
        m_ref[h], l_ref[h], acc_ref[h] = _fox_update(s2, v_h, m_ref[h], l_ref[h], acc_ref[h])

    @pl.when(kb == pl.num_programs(1) - 1)
    def _():
        for h in range(heads):
            o_ref[:, _head_cols(h)] = (acc_ref[h] / l_ref[h]).astype(o_ref.dtype)


def fox_decode(q, k_new, v_new, k_cache, v_cache, cq, ck_cache, ck_new, heads, tk=1024):
    b, tq, hd = q.shape
    n_new = k_new.shape[1]
    p_len = k_cache.shape[1] // heads
    tk = min(tk, p_len)
    return pl.pallas_call(
        functools.partial(_fox_decode_kernel, heads=heads, tk=tk),
        grid=(b, p_len // tk),
        in_specs=[
            pl.BlockSpec((None, tq, hd), lambda bi, kb: (bi, 0, 0)),
            pl.BlockSpec((None, n_new, hd), lambda bi, kb: (bi, 0, 0)),
            pl.BlockSpec((None, n_new, hd), lambda bi, kb: (bi, 0, 0)),
            pl.BlockSpec((None, tk * heads, HEAD_DIM), lambda bi, kb: (bi, kb, 0)),
            pl.BlockSpec((None, tk * heads, HEAD_DIM), lambda bi, kb: (bi, kb, 0)),
            pl.BlockSpec((None, heads, tq, LANES), lambda bi, kb: (bi, 0, 0, 0)),
            pl.BlockSpec((None, heads, 1, tk), lambda bi, kb: (bi, 0, 0, kb)),
            pl.BlockSpec((None, heads, 1, n_new), lambda bi, kb: (bi, 0, 0, 0)),
        ],
        out_specs=pl.BlockSpec((None, tq, hd), lambda bi, kb: (bi, 0, 0)),
        out_shape=jax.ShapeDtypeStruct((b, tq, hd), BF16),
        scratch_shapes=[pltpu.VMEM((heads, tq, LANES), F32), pltpu.VMEM((heads, tq, LANES), F32),
                        pltpu.VMEM((heads, tq, HEAD_DIM), F32)],
        compiler_params=_cparams(("parallel", "arbitrary")),
        name="fox_decode",
    )(q, k_new, v_new, k_cache, v_cache, cq, ck_cache, ck_new)


SB_DEAD = -160.0


def _suffix_sums(sp, ntri, carry):
    width = ntri.shape[0]
    n = sp.shape[1]
    parts = []
    for c0 in range(n - width, -1, -width):
        piece = sp[:, c0:c0 + width]
        hi = piece.astype(BF16)
        lo = (piece - hi.astype(F32)).astype(BF16)
        c = _dot(hi, ntri) + _dot(lo, ntri) + carry
        carry = c[:, 0:1]
        parts.append(c)
    parts.reverse()
    return (parts[0] if len(parts) == 1 else jnp.concatenate(parts, axis=1)), carry


def _mask_tail(x, mask):
    if mask is None:
        return x
    head = x.shape[1] - mask.shape[1]
    tail = jnp.where(mask, x[:, head:], 0.0)
    return tail if head == 0 else jnp.concatenate([x[:, :head], tail], axis=1)


def _sb_weights(z2, ntri, carry, mask):
    sp = _mask_tail(_softplus2(z2), mask)
    c, new_carry = _suffix_sums(sp, ntri, carry)
    a = _mask_tail(jnp.exp2(z2 + c), mask)
    return a.astype(BF16), new_carry


def _sb_prompt_kernel(q_ref, k_ref, v_ref, ntri_ref, o_ref, r_ref, acc_ref, *, tq, bs):
    qi = pl.program_id(1)
    nsub = tq // bs
    ntri = ntri_ref[...]
    row = lax.broadcasted_iota(jnp.int32, (bs, bs), 0)
    col = lax.broadcasted_iota(jnp.int32, (bs, bs), 1)

    def sub_rows(s):
        return slice(s * bs, (s + 1) * bs)

    def block(s, j, nblk, carry, mask):
        rows = pl.ds(pl.multiple_of(j * bs, bs), nblk * bs)
        a, carry = _sb_weights(_dot_nt(q_ref[sub_rows(s), :], k_ref[rows, :]), ntri, carry, mask)
        return _dot(a, v_ref[rows, :]), carry

    def first_units(at_origin):
        for s in range(nsub):
            if s == 0 and at_origin:
                out, carry = block(s, 0, 1, jnp.zeros((bs, 1), F32), col < row)
            else:
                out, carry = block(s, qi * nsub + s - 1, 2, jnp.zeros((bs, 1), F32), col < row)
            acc_ref[s] = out
            r_ref[s] = jnp.broadcast_to(carry, (bs, LANES))

    @pl.when(qi == 0)
    def _():
        first_units(True)

    @pl.when(qi > 0)
    def _():
        first_units(False)

    def live(s):
        return jnp.max(r_ref[s]) > SB_DEAD

    alive_after_first = [live(s) for s in range(nsub)]

    def rest(s):
        g = qi * nsub + s

        def step(j, nblk):
            out, carry = block(s, j, nblk, r_ref[s][:, 0:1], None)
            acc_ref[s] += out
            r_ref[s] = jnp.broadcast_to(carry, (bs, LANES))

        n_rest = jnp.maximum(g - 1, 0)

        def cond(state):
            i, alive = state
            return jnp.logical_and(i < n_rest // 4, alive)

        def body(state):
            i, _ = state
            step(n_rest - 4 - 4 * i, 4)
            return i + 1, live(s)

        _, alive = lax.while_loop(cond, body, (0, alive_after_first[s]))

        @pl.when(jnp.logical_and(n_rest % 4 >= 2, alive))
        def _():
            step(n_rest % 2, 2)

        @pl.when(jnp.logical_and(n_rest % 2 == 1, alive))
        def _():
            step(0, 1)

        o_ref[sub_rows(s), :] = acc_ref[s].astype(o_ref.dtype)

    for s in range(nsub):
        rest(s)


def sb_prompt(q, k, v, ntri, tq=1024, bs=256):
    t, hd = q.shape
    h = hd // HEAD_DIM
    tq = min(tq, t)
    bs = min(bs, tq)
    return pl.pallas_call(
        functools.partial(_sb_prompt_kernel, tq=tq, bs=bs),
        grid=(h, t // tq),
        in_specs=[
            pl.BlockSpec((tq, HEAD_DIM), lambda hi, qi: (qi, hi)),
            pl.BlockSpec((t, HEAD_DIM), lambda hi, qi: (0, hi)),
            pl.BlockSpec((t, HEAD_DIM), lambda hi, qi: (0, hi)),
            _resident(ntri.shape, lambda hi, qi: (0, 0)),
        ],
        out_specs=pl.BlockSpec((tq, HEAD_DIM), lambda hi, qi: (qi, hi)),
        out_shape=jax.ShapeDtypeStruct((t, hd), BF16),
        scratch_shapes=[pltpu.VMEM((tq // bs, bs, LANES), F32), pltpu.VMEM((tq // bs, bs, HEAD_DIM), F32)],
        compiler_params=_cparams(("parallel", "arbitrary")),
        name="sb_prompt",
    )(q, k, v, ntri)


def _sb_decode_kernel(q_ref, kn_ref, vn_ref, kc_hbm, vc_hbm, ntri_ref, o_ref, kbuf, vbuf, sem, r_ref, acc_ref,
                      *, heads, tk):
    b = pl.program_id(0)
    tq = q_ref.shape[0]
    n_new = kn_ref.shape[0]
    blk = tk * heads
    nkb = kc_hbm.shape[1] // blk

    def fetch(j, slot):
        rows = pl.ds(pl.multiple_of(j * blk, blk), blk)
        return (pltpu.make_async_copy(kc_hbm.at[b, rows, :], kbuf.at[slot], sem.at[0, slot]),
                pltpu.make_async_copy(vc_hbm.at[b, rows, :], vbuf.at[slot], sem.at[1, slot]))

    for copy in fetch(nkb - 1, 0):
        copy.start()

    def head_rows(h):
        return slice(h * tq, (h + 1) * tq)

    def stacked_scores(keys_of):
        return jnp.concatenate([_dot_nt(q_ref[:, _head_cols(h)], keys_of(h)) for h in range(heads)], axis=0)

    def live():
        return jnp.max(r_ref[...]) > SB_DEAD

    z2 = stacked_scores(lambda h: kn_ref[:, _head_cols(h)])
    row = lax.rem(lax.broadcasted_iota(jnp.int32, (heads * tq, n_new), 0), tq)
    col = lax.broadcasted_iota(jnp.int32, (heads * tq, n_new), 1)
    a, carry = _sb_weights(z2, ntri_ref[0:n_new, 0:n_new], jnp.zeros((heads * tq, 1), F32), col < row)
    for h in range(heads):
        acc_ref[head_rows(h), :] = _dot(a[head_rows(h), :], vn_ref[:, _head_cols(h)])
    r_ref[...] = jnp.broadcast_to(carry, r_ref.shape)

    def body(state):
        i, _ = state
        j = nkb - 1 - i
        slot = lax.rem(i, 2)
        for copy in fetch(j, slot):
            copy.wait()

        @pl.when(j > 0)
        def _():
            for copy in fetch(j - 1, 1 - slot):
                copy.start()

        z2 = stacked_scores(lambda h: _head_rows(kbuf.at[slot], h, heads, tk).astype(BF16))
        a, carry = _sb_weights(z2, ntri_ref[...], r_ref[:, 0:1], None)
        for h in range(heads):
            acc_ref[head_rows(h), :] += _dot(a[head_rows(h), :], _head_rows(vbuf.at[slot], h, heads, tk).astype(BF16))
        r_ref[...] = jnp.broadcast_to(carry, r_ref.shape)
        return i + 1, live()

    n_done, _ = lax.while_loop(lambda st: jnp.logical_and(st[0] < nkb, st[1]), body, (0, live()))

    @pl.when(n_done < nkb)
    def _():
        for copy in fetch(nkb - 1 - n_done, lax.rem(n_done, 2)):
            copy.wait()

    for h in range(heads):
        o_ref[:, _head_cols(h)] = acc_ref[head_rows(h), :].astype(o_ref.dtype)


def sb_decode(q, k_new, v_new, k_cache, v_cache, ntri, heads, tk=512):
    b, tq, hd = q.shape
    n_new = k_new.shape[1]
    p_len = k_cache.shape[1] // heads
    tk = min(tk, p_len)
    return pl.pallas_call(
        functools.partial(_sb_decode_kernel, heads=heads, tk=tk),
        grid=(b,),
        in_specs=[
            pl.BlockSpec((None, tq, hd), lambda bi: (bi, 0, 0)),
            pl.BlockSpec((None, n_new, hd), lambda bi: (bi, 0, 0)),
            pl.BlockSpec((None, n_new, hd), lambda bi: (bi, 0, 0)),
            pl.BlockSpec(memory_space=pl.ANY),
            pl.BlockSpec(memory_space=pl.ANY),
            _resident(ntri.shape, lambda bi: (0, 0)),
        ],
        out_specs=pl.BlockSpec((None, tq, hd), lambda bi: (bi, 0, 0)),
        out_shape=jax.ShapeDtypeStruct((b, tq, hd), BF16),
        scratch_shapes=[pltpu.VMEM((2, tk * heads, HEAD_DIM), F32), pltpu.VMEM((2, tk * heads, HEAD_DIM), F32),
                        pltpu.SemaphoreType.DMA((2, 2)),
                        pltpu.VMEM((heads * tq, LANES), F32), pltpu.VMEM((heads * tq, HEAD_DIM), F32)],
        compiler_params=_cparams(("arbitrary",)),
        name="sb_decode",
    )(q, k_new, v_new, k_cache, v_cache, ntri)


def _out_proj_kernel(xa_ref, xb_ref, w_ref, h_ref, o_ref):
    ka = xa_ref.shape[1]
    o_ref[...] = h_ref[...] + _dot(xa_ref[...], w_ref[:ka, :]) + _dot(xb_ref[...], w_ref[ka:, :])


def out_proj(xa, xb, col_b, w, h, tm=512):
    m, d = h.shape
    ka = w.shape[0] // 2
    tm = min(tm, m)
    return pl.pallas_call(
        _out_proj_kernel,
        grid=(m // tm,),
        in_specs=[
            pl.BlockSpec((tm, ka), lambda i: (i, 0)),
            pl.BlockSpec((tm, ka), lambda i: (i, col_b)),
            _resident(w.shape, lambda i: (0, 0)),
            pl.BlockSpec((tm, d), lambda i: (i, 0)),
        ],
        out_specs=pl.BlockSpec((tm, d), lambda i: (i, 0)),
        out_shape=jax.ShapeDtypeStruct((m, d), F32),
        compiler_params=_cparams(("parallel",)),
        name="out_proj",
    )(xa, xb, w, h)


def _ffn_kernel(h_ref, g_ref, wg_ref, wu_ref, wd_ref, gf_ref, o_ref, xn_ref, *, final_norm):
    fi = pl.program_id(1)

    @pl.when(fi == 0)
    def _():
        xn_ref[...] = _rms(h_ref[...], g_ref[...]).astype(BF16)
        o_ref[...] = jnp.zeros_like(o_ref)

    xn = xn_ref[...]
    gate = _dot(xn, wg_ref[...])
    up = _dot(xn, wu_ref[...])
    mid = (gate / (1.0 + jnp.exp(-gate)) * up).astype(BF16)
    width = wd_ref.shape[0]
    for c0 in range(0, o_ref.shape[1], width):
        o_ref[:, c0:c0 + width] += _dot(mid, wd_ref[:, c0:c0 + width])

    @pl.when(fi == pl.num_programs(1) - 1)
    def _():
        hn = h_ref[...] + o_ref[...]
        o_ref[...] = _rms(hn, gf_ref[...]) if final_norm else hn


def ffn(h, g, w_gate, w_up, w_down, layer, g_final, final_norm, tm=1024, tf=512):
    m, d = h.shape
    f = w_gate.shape[2]
    tm = min(tm, m)
    return pl.pallas_call(
        functools.partial(_ffn_kernel, final_norm=final_norm),
        grid=(m // tm, f // tf),
        in_specs=[
            pl.BlockSpec((tm, d), lambda i, j: (i, 0), pipeline_mode=pl.Buffered(1)),
            _resident((1, d), lambda i, j: (0, 0)),
            pl.BlockSpec((None, d, tf), lambda i, j: (layer, 0, j)),
            pl.BlockSpec((None, d, tf), lambda i, j: (layer, 0, j)),
            pl.BlockSpec((None, tf, d), lambda i, j: (layer, j, 0)),
            _resident((1, d), lambda i, j: (0, 0)),
        ],
        out_specs=pl.BlockSpec((tm, d), lambda i, j: (i, 0)),
        out_shape=jax.ShapeDtypeStruct((m, d), F32),
        scratch_shapes=[pltpu.VMEM((tm, d), BF16)],
        compiler_params=_cparams(("parallel", "arbitrary")),
        name="ffn",
    )(h, g, w_gate, w_up, w_down, g_final)


def _pad_rows(x, rows):
    return jnp.pad(x, ((0, 0), (0, rows - x.shape[1]), (0, 0)))


def _neg_tri_matrix():
    idx = np.arange(TRI)
    return jnp.asarray(-(idx[:, None] >= idx[None, :]).astype(np.float32), dtype=BF16)


def _layer0_groups(pool_w, fox_w):
    outs = [("f32",), ("qscaled",), ("f32", "bf16"), ("f32", "bf16")]
    groups, out_defs, col = [], [], 0
    for kinds in outs:
        width = pool_w if col == 0 else fox_w
        o = []
        for kind in kinds:
            o.append((len(out_defs), 0, kind))
            out_defs.append((width, F32 if kind == "f32" else BF16))
        groups.append((col, width, tuple(o)))
        col += width
    groups.append((None, LANES, ((len(out_defs), 0, "logsig"),)))
    out_defs.append((LANES, F32))
    return tuple(groups), out_defs


def _layer1_groups(d):
    half = d // 2
    groups = []
    for part, outs in enumerate([(0,), (1, 2), (3, 4)]):
        for c in range(2):
            kinds = ("qscaled",) if part == 0 else ("f32", "bf16")
            groups.append((part * d + c * half, half,
                           tuple((oi, c * half, kind) for oi, kind in zip(outs, kinds))))
    out_defs = [(d, BF16), (d, F32), (d, BF16), (d, F32), (d, BF16)]
    return tuple(groups), out_defs


def _run_trunk(x, n_past, pool_c, fk_c, fv_c, flf_c, sk_c, sv_c, wts):
    b, t, d = x.shape
    m = b * t
    h = x.reshape(m, d)
    pool_w = wts["w_pool"].shape[0] * wts["w_pool"].shape[1]
    fox_w = d - pool_w
    fox_h = fox_w // HEAD_DIM
    sb_h = d // HEAD_DIM
    first = fk_c is None
    zero_bias = jnp.zeros((1, LANES), F32)

    groups, out_defs = _layer0_groups(pool_w, fox_w)
    u, q, k, kb, v, vb, lf = norm_proj(h, wts["ln_mix0"], wts["w_in"], wts["w_forget"], wts["b_forget"], groups,
                                        out_defs, tm=256)
    log_f = lf[:, :fox_h]
    if first:
        prefix = jnp.zeros((b, POOL_HALO, pool_w), F32)
    else:
        prefix = jnp.pad(pool_c, ((0, 0), (POOL_HALO - POOL_STATE, 0), (0, 0)))
    y_pool = pool_mixer(u.reshape(b, t, pool_w), prefix, wts["w_pool"], wts["pool_scale"], n_past)
    if t >= POOL_STATE:
        pool_state = u.reshape(b, t, pool_w)[:, t - POOL_STATE:]
    else:
        pool_state = jnp.concatenate([prefix[:, 1:], u.reshape(b, t, pool_w)], axis=1)[:, -POOL_STATE:]

    if first:
        bs = min(512, t)
        tq = min(2 * bs, t)
        cum2 = cumsum_rows(lf)[:, :fox_h].T * LOG2E
        ck = cum2.reshape(fox_h, t // bs, 1, bs)
        o = fox_prompt(q, kb, vb, ck, ck[:, :, 0, bs - 1], tq, bs)
    else:
        n_new = LANES
        lf_new = log_f.reshape(b, t, fox_h).transpose(1, 0, 2).reshape(t, b * fox_h)
        lf_all = jnp.concatenate([flf_c.transpose(1, 0, 2).reshape(n_past, b * fox_h), lf_new], axis=0)
        rows = n_past + t
        rows_pad = -(-rows // 512) * 512
        lf_all = jnp.pad(lf_all, ((0, rows_pad - rows), (0, LANES - b * fox_h)))
        cum2 = cumsum_rows(lf_all)[:rows, :b * fox_h].reshape(rows, b, fox_h).transpose(1, 2, 0) * LOG2E
        ck_cache = cum2[:, :, None, :n_past]
        cum_new = cum2[:, :, n_past:]
        ck_new = jnp.pad(cum_new, ((0, 0), (0, 0), (0, n_new - t)))[:, :, None, :]
        cq = jnp.broadcast_to(cum_new[:, :, :, None], (b, fox_h, t, LANES))
        o = fox_decode(q.reshape(b, t, fox_w), _pad_rows(kb.reshape(b, t, fox_w), n_new),
                       _pad_rows(vb.reshape(b, t, fox_w), n_new), fk_c, fv_c, cq, ck_cache, ck_new, fox_h)
        o = o.reshape(m, fox_w)
    h = out_proj(y_pool.reshape(m, pool_w), o, 0, wts["w_out_ab"], h)
    h = ffn(h, wts["ln_ffn0"], wts["w_gate"], wts["w_up"], wts["w_down"], 0, wts["ln_final"], False)

    groups1, out_defs1 = _layer1_groups(d)
    q1, k1, k1b, v1, v1b = norm_proj(h, wts["ln_mix1"], wts["w_qkv"], wts["w_forget"], zero_bias, groups1, out_defs1,
                                     tm=256)
    ntri = _neg_tri_matrix()
    if first:
        o1 = sb_prompt(q1, k1b, v1b, ntri)
    else:
        n_new = LANES
        o1 = sb_decode(q1.reshape(b, t, d), _pad_rows(k1b.reshape(b, t, d), n_new),
                       _pad_rows(v1b.reshape(b, t, d), n_new), sk_c, sv_c, ntri, sb_h).reshape(m, d)
    h = out_proj(o1, o1, 1, wts["w_out_sb"], h)
    y = ffn(h, wts["ln_ffn1"], wts["w_gate"], wts["w_up"], wts["w_down"], 1, wts["ln_final"], True)

    return (y.reshape(b, t, d), pool_state[None], k.reshape(1, b, t, fox_h, HEAD_DIM),
            v.reshape(1, b, t, fox_h, HEAD_DIM), log_f.reshape(1, b, t, fox_h),
            k1.reshape(1, b, t, sb_h, HEAD_DIM), v1.reshape(1, b, t, sb_h, HEAD_DIM))


def kernel(x_prompt, x_sample, cache_pool, cache_fox_k, cache_fox_v, cache_fox_logf, cache_sb_k, cache_sb_v, ln_mix, w_in_ab, b_forget, w_pool, pool_scale, w_out_ab, w_qkv_sb, w_out_sb, ln_ffn, w_gate, w_up, w_down, ln_final):
    fox_h = b_forget.shape[-1]
    n_main = w_in_ab.shape[-1] - fox_h
    wts = {
        "ln_mix0": ln_mix[0][None], "ln_mix1": ln_mix[1][None],
        "ln_ffn0": ln_ffn[0][None], "ln_ffn1": ln_ffn[1][None], "ln_final": ln_final[None],
        "w_in": w_in_ab[0, :, :n_main].astype(BF16),
        "w_forget": jnp.pad(w_in_ab[0, :, n_main:], ((0, 0), (0, LANES - fox_h))).astype(BF16),
        "b_forget": jnp.pad(b_forget[0], (0, LANES - fox_h))[None],
        "w_pool": w_pool[0].astype(BF16), "pool_scale": pool_scale[0][None],
        "w_out_ab": w_out_ab[0].astype(BF16), "w_qkv": w_qkv_sb[0].astype(BF16),
        "w_out_sb": w_out_sb[0].astype(BF16),
        "w_gate": w_gate.astype(BF16), "w_up": w_up.astype(BF16), "w_down": w_down.astype(BF16),
    }
    out_p = _run_trunk(x_prompt, 0, None, None, None, None, None, None, wts)
    n_past = cache_fox_k.shape[2]
    bs = x_sample.shape[0]

    def rows_of(cache):
        return cache.reshape(bs, -1, HEAD_DIM)

    out_s = _run_trunk(x_sample, n_past, cache_pool[0], rows_of(cache_fox_k), rows_of(cache_fox_v),
                       cache_fox_logf[0], rows_of(cache_sb_k), rows_of(cache_sb_v), wts)
    return (out_p[0], out_s[0]) + out_p[1:] + out_s[1:]
```

```python
import functools

import numpy as np
import jax
import jax.numpy as jnp
from jax import lax
from jax.experimental import pallas as pl
from jax.experimental.pallas import tpu as pltpu

HEAD_DIM = 128
EPS = 1e-6
POOL_WINDOWS = (2, 4, 8, 16)
POOL_STATE = max(POOL_WINDOWS) - 1
POOL_HALO = 16
LANES = 128
TRI = 256
VMEM_LIMIT = 56 * 1024 * 1024
LOG2E = 1.4426950408889634
QK_SCALE2 = HEAD_DIM ** -0.5 * LOG2E

F32 = jnp.float32
BF16 = jnp.bfloat16


def _cparams(sem, vmem=VMEM_LIMIT):
    return pltpu.CompilerParams(dimension_semantics=sem, vmem_limit_bytes=vmem)


def _resident(shape, index_map):
    return pl.BlockSpec(shape, index_map, pipeline_mode=pl.Buffered(1))


def _dot(a, b):
    return jnp.dot(a, b, preferred_element_type=F32)


def _dot_nt(a, b):
    return lax.dot_general(a, b, (((1,), (1,)), ((), ())), preferred_element_type=F32)


def _rms(x, g):
    ms = jnp.mean(x * x, axis=-1, keepdims=True)
    return x * lax.rsqrt(ms + EPS) * g


def _softplus(z):
    return jnp.maximum(z, 0.0) + jnp.log1p(jnp.exp(-jnp.abs(z)))


def _softplus2(z2):
    return jnp.maximum(z2, 0.0) + jnp.log(1.0 + jnp.exp2(-jnp.abs(z2))) * LOG2E


def _tile_lanes(x, n):
    reps = n // LANES
    return x if reps == 1 else jnp.concatenate([x] * reps, axis=1)


def _head_cols(h):
    return slice(h * HEAD_DIM, (h + 1) * HEAD_DIM)


def _head_rows(ref, h, heads, n):
    return ref[pl.ds(h, n, stride=heads), :]


def _norm_proj_kernel(x_ref, g_ref, w_ref, w2_ref, b_ref, *out_refs, groups):
    xn = _rms(x_ref[...], g_ref[...]).astype(BF16)
    for w_col, width, outs in groups:
        acc = _dot(xn, w2_ref[...] if w_col is None else w_ref[:, w_col:w_col + width])
        for out_idx, out_col, kind in outs:
            if kind == "logsig":
                val = -_softplus(-(acc + b_ref[...]))
            elif kind == "qscaled":
                val = acc * QK_SCALE2
            else:
                val = acc
            o = out_refs[out_idx]
            o[:, out_col:out_col + width] = val.astype(o.dtype)


def norm_proj(x, g, w, w2, bias, groups, out_defs, tm):
    m, d = x.shape
    n = w.shape[1]
    tm = min(tm, m)
    out_shape = [jax.ShapeDtypeStruct((m, c), dt) for c, dt in out_defs]
    out_specs = [pl.BlockSpec((tm, c), lambda i: (i, 0)) for c, _ in out_defs]
    return pl.pallas_call(
        functools.partial(_norm_proj_kernel, groups=groups),
        grid=(m // tm,),
        in_specs=[
            pl.BlockSpec((tm, d), lambda i: (i, 0)),
            _resident((1, d), lambda i: (0, 0)),
            _resident((d, n), lambda i: (0, 0)),
            _resident((d, LANES), lambda i: (0, 0)),
            _resident((1, LANES), lambda i: (0, 0)),
        ],
        out_specs=out_specs,
        out_shape=out_shape,
        compiler_params=_cparams(("parallel",)),
        name="norm_proj",
    )(x, g, w, w2, bias)


def _cumsum_kernel(x_ref, o_ref, carry_ref, *, tb):
    @pl.when(pl.program_id(0) == 0)
    def _():
        carry_ref[...] = jnp.zeros_like(carry_ref)

    x = x_ref[...]
    row = lax.broadcasted_iota(jnp.int32, (tb, tb), 0)
    col = lax.broadcasted_iota(jnp.int32, (tb, tb), 1)
    low = jnp.where(col <= row, 1.0, 0.0).astype(BF16)
    hi = x.astype(BF16)
    r1 = x - hi.astype(F32)
    mid = r1.astype(BF16)
    lo = (r1 - mid.astype(F32)).astype(BF16)
    out = _dot(low, hi) + _dot(low, mid) + _dot(low, lo) + carry_ref[0:1, :]
    o_ref[...] = out
    carry_ref[...] = jnp.broadcast_to(out[tb - 1:tb, :], carry_ref.shape)


def cumsum_rows(x, tb=512):
    t, c = x.shape
    tb = min(tb, t)
    return pl.pallas_call(
        functools.partial(_cumsum_kernel, tb=tb),
        grid=(t // tb,),
        in_specs=[pl.BlockSpec((tb, c), lambda i: (i, 0))],
        out_specs=pl.BlockSpec((tb, c), lambda i: (i, 0)),
        out_shape=jax.ShapeDtypeStruct((t, c), F32),
        scratch_shapes=[pltpu.VMEM((8, c), F32)],
        compiler_params=_cparams(("arbitrary",)),
        name="cumsum_rows",
    )(x)


def _pool_kernel(u_ref, pre_ref, w_ref, sc_ref, o_ref, buf_ref, *, tm, pos0, group):
    mi = pl.program_id(1)

    @pl.when(mi == 0)
    def _():
        buf_ref[0:POOL_HALO, :] = pre_ref[...]

    @pl.when(mi > 0)
    def _():
        buf_ref[0:POOL_HALO, :] = buf_ref[tm:tm + POOL_HALO, :]

    buf_ref[POOL_HALO:POOL_HALO + tm, :] = u_ref[...]
    pos = pos0 + mi * tm + lax.broadcasted_iota(jnp.int32, (tm, 1), 0)
    for gi, win in enumerate(POOL_WINDOWS):
        cs = slice(gi * group, (gi + 1) * group)
        tot = buf_ref[POOL_HALO:POOL_HALO + tm, cs]
        for back in range(1, win):
            tot = tot + buf_ref[POOL_HALO - back:POOL_HALO - back + tm, cs]
        count = jnp.minimum(pos + 1, win).astype(F32)
        diff = tot / count - u_ref[:, cs]
        y = _dot(diff.astype(BF16), w_ref[gi]) * sc_ref[:, cs]
        o_ref[:, cs] = y.astype(o_ref.dtype)


def pool_mixer(u, prefix, w_pool, scale, pos0, tm=512):
    b, t, c = u.shape
    tm = min(tm, t)
    group = c // len(POOL_WINDOWS)
    return pl.pallas_call(
        functools.partial(_pool_kernel, tm=tm, pos0=pos0, group=group),
        grid=(b, t // tm),
        in_specs=[
            pl.BlockSpec((None, tm, c), lambda bi, mi: (bi, mi, 0)),
            pl.BlockSpec((None, POOL_HALO, c), lambda bi, mi: (bi, 0, 0)),
            _resident(w_pool.shape, lambda bi, mi: (0, 0, 0)),
            _resident((1, c), lambda bi, mi: (0, 0)),
        ],
        out_specs=pl.BlockSpec((None, tm, c), lambda bi, mi: (bi, mi, 0)),
        out_shape=jax.ShapeDtypeStruct((b, t, c), BF16),
        scratch_shapes=[pltpu.VMEM((POOL_HALO + tm, c), F32)],
        compiler_params=_cparams(("arbitrary", "arbitrary")),
        name="pool_mixer",
    )(u, prefix, w_pool, scale)


def _fox_update(s2, v, m_prev, l_prev, acc_prev):
    m_new = jnp.maximum(m_prev, jnp.max(s2, axis=1, keepdims=True))
    alpha = jnp.exp2(m_prev - m_new)
    p = jnp.exp2(s2 - _tile_lanes(m_new, s2.shape[1]))
    l_new = alpha * l_prev + jnp.sum(p, axis=1, keepdims=True)
    acc_new = alpha * acc_prev + _dot(p.astype(BF16), v)
    return m_new, l_new, acc_new


FOX_DEAD = -160.0


def _fox_prompt_kernel(ck_end_ref, q_ref, k_ref, v_ref, ck_ref, o_ref, m_ref, l_ref, acc_ref, cq_ref, kmax_ref,
                       *, tq, bs):
    hi = pl.program_id(0)
    qi = pl.program_id(1)
    nq = pl.num_programs(1)
    nsub = tq // bs

    @pl.when(qi == 0)
    def _():
        def norm_body(j, best):
            kb = k_ref[pl.ds(pl.multiple_of(j * tq, tq), tq), :].astype(F32)
            return jnp.maximum(best, jnp.sum(kb * kb, axis=1, keepdims=True))

        best = lax.fori_loop(0, nq, norm_body, jnp.zeros((tq, 1), F32))
        kmax_ref[...] = jnp.broadcast_to(jnp.sqrt(jnp.max(best, axis=0, keepdims=True)), kmax_ref.shape)

    row = lax.broadcasted_iota(jnp.int32, (bs, bs), 0)
    col = lax.broadcasted_iota(jnp.int32, (bs, bs), 1)

    def sub_rows(s):
        return slice(s * bs, (s + 1) * bs)

    for s in range(nsub):
        own = jnp.where(row == col, jnp.broadcast_to(ck_ref[qi * nsub + s], (bs, bs)), 0.0)
        cq_ref[sub_rows(s), :] = jnp.broadcast_to(jnp.sum(own, axis=1, keepdims=True), (bs, LANES))

    def update(s, j, nblk, causal, fresh=False):
        width = nblk * bs
        rows = pl.ds(pl.multiple_of(j * bs, bs), width)
        cq = cq_ref[sub_rows(s), :]
        ck = ck_ref[j] if nblk == 1 else jnp.concatenate([ck_ref[j + i] for i in range(nblk)], axis=1)
        s2 = _dot_nt(q_ref[sub_rows(s), :], k_ref[rows, :]) + _tile_lanes(cq, width) - ck
        if causal:
            tail = jnp.where(col <= row, s2[:, width - bs:], -jnp.inf)
            s2 = tail if nblk == 1 else jnp.concatenate([s2[:, :width - bs], tail], axis=1)
        if fresh:
            state = (jnp.full((bs, LANES), -jnp.inf, F32), jnp.zeros((bs, LANES), F32), jnp.zeros((bs, HEAD_DIM), F32))
        else:
            state = (m_ref[s], l_ref[s], acc_ref[s])
        m_ref[s], l_ref[s], acc_ref[s] = _fox_update(s2, v_ref[rows, :], *state)

    def first_units(at_origin):
        for s in range(nsub):
            if s == 0 and at_origin:
                update(s, 0, 1, True, fresh=True)
            else:
                update(s, qi * nsub + s - 1, 2, True, fresh=True)

    @pl.when(qi == 0)
    def _():
        first_units(True)

    @pl.when(qi > 0)
    def _():
        first_units(False)

    def rest(s):
        g = qi * nsub + s
        qf = q_ref[sub_rows(s), :].astype(F32)
        reach = (jnp.sqrt(jnp.sum(qf * qf, axis=1, keepdims=True)) * kmax_ref[0:1, 0:1]
                 + cq_ref[sub_rows(s), 0:1])

        def margin():
            return jnp.max(reach - m_ref[s][:, 0:1])

        def live(gap, j_last):
            return gap - ck_end_ref[hi, jnp.maximum(j_last, 0)] > FOX_DEAD

        n_rest = jnp.maximum(g - 1, 0)
        gap0 = margin()
        pair = n_rest >= 2

        @pl.when(jnp.logical_and(pair, live(gap0, n_rest - 1)))
        def _():
            update(s, n_rest - 2, 2, False)

        n_left = jnp.where(pair, n_rest - 2, n_rest)

        def cond(state):
            i, alive = state
            return jnp.logical_and(i < n_left // 4, alive)

        def body(state):
            i, _ = state
            update(s, n_left - 4 - 4 * i, 4, False)
            return i + 1, live(margin(), n_left - 5 - 4 * i)

        _, alive = lax.while_loop(cond, body, (0, live(gap0, n_left - 1)))
        rem = n_left % 4

        @pl.when(jnp.logical_and(rem >= 2, alive))
        def _():
            update(s, rem % 2, 2, False)

        @pl.when(jnp.logical_and(rem % 2 == 1, alive))
        def _():
            update(s, 0, 1, False)

        o_ref[sub_rows(s), :] = (acc_ref[s] / l_ref[s]).astype(o_ref.dtype)

    for s in range(nsub):
        rest(s)


def fox_prompt(q, k, v, ck, ck_end, tq, bs):
    t, hd = q.shape
    h = hd // HEAD_DIM
    nq = t // tq
    return pl.pallas_call(
        functools.partial(_fox_prompt_kernel, tq=tq, bs=bs),
        grid=(h, nq),
        in_specs=[
            pl.BlockSpec(memory_space=pltpu.SMEM),
            pl.BlockSpec((tq, HEAD_DIM), lambda hi, qi: (qi, hi)),
            pl.BlockSpec((t, HEAD_DIM), lambda hi, qi: (0, hi)),
            pl.BlockSpec((t, HEAD_DIM), lambda hi, qi: (0, hi)),
            pl.BlockSpec((None, t // bs, 1, bs), lambda hi, qi: (hi, 0, 0, 0)),
        ],
        out_specs=pl.BlockSpec((tq, HEAD_DIM), lambda hi, qi: (qi, hi)),
        out_shape=jax.ShapeDtypeStruct((t, hd), BF16),
        scratch_shapes=[pltpu.VMEM((tq // bs, bs, LANES), F32), pltpu.VMEM((tq // bs, bs, LANES), F32),
                        pltpu.VMEM((tq // bs, bs, HEAD_DIM), F32), pltpu.VMEM((tq, LANES), F32),
                        pltpu.VMEM((8, LANES), F32)],
        compiler_params=_cparams(("arbitrary", "arbitrary")),
        name="fox_prompt",
    )(ck_end, q, k, v, ck)


def _fox_decode_kernel(q_ref, kn_ref, vn_ref, kc_ref, vc_ref, cq_ref, ckc_ref, ckn_ref, o_ref,
                       m_ref, l_ref, acc_ref, *, heads, tk):
    kb = pl.program_id(1)
    tq = q_ref.shape[0]
    n_new = kn_ref.shape[0]

    @pl.when(kb == 0)
    def _():
        row = lax.broadcasted_iota(jnp.int32, (tq, n_new), 0)
        col = lax.broadcasted_iota(jnp.int32, (tq, n_new), 1)
        for h in range(heads):
            hc = _head_cols(h)
            s2 = _dot_nt(q_ref[:, hc], kn_ref[:, hc]) + _tile_lanes(cq_ref[h], n_new) - ckn_ref[h]
            s2 = jnp.where(col <= row, s2, -jnp.inf)
            m = jnp.max(s2, axis=1, keepdims=True)
            p = jnp.exp2(s2 - m)
            m_ref[h] = jnp.broadcast_to(m, (tq, LANES))
            l_ref[h] = jnp.broadcast_to(jnp.sum(p, axis=1, keepdims=True), (tq, LANES))
            acc_ref[h] = _dot(p.astype(BF16), vn_ref[:, hc])

    for h in range(heads):
        k_h = _head_rows(kc_ref, h, heads, tk).astype(BF16)
        v_h = _head_rows(vc_ref, h, heads, tk).astype(BF16)
        s2 = _dot_nt(q_ref[:, _head_cols(h)], k_h) + _tile_lanes(cq_ref[h], tk) - ckc_ref[h]
        m_ref[h], l_ref[h], acc_ref[h] = _fox_update(s2, v_h, m_ref[h], l_ref[h], acc_ref[h])

    @pl.when(kb == pl.num_programs(1) - 1)
    def _():
        for h in range(heads):
            o_ref[:, _head_cols(h)] = (acc_ref[h] / l_ref[h]).astype(o_ref.dtype)


def fox_decode(q, k_new, v_new, k_cache, v_cache, cq, ck_cache, ck_new, heads, tk=2048):
    b, tq, hd = q.shape
    n_new = k_new.shape[1]
    p_len = k_cache.shape[1] // heads
    tk = min(tk, p_len)
    return pl.pallas_call(
        functools.partial(_fox_decode_kernel, heads=heads, tk=tk),
        grid=(b, p_len // tk),
        in_specs=[
            pl.BlockSpec((None, tq, hd), lambda bi, kb: (bi, 0, 0)),
            pl.BlockSpec((None, n_new, hd), lambda bi, kb: (bi, 0, 0)),
            pl.BlockSpec((None, n_new, hd), lambda bi, kb: (bi, 0, 0)),
            pl.BlockSpec((None, tk * heads, HEAD_DIM), lambda bi, kb: (bi, kb, 0)),
            pl.BlockSpec((None, tk * heads, HEAD_DIM), lambda bi, kb: (bi, kb, 0)),
            pl.BlockSpec((None, heads, tq, LANES), lambda bi, kb: (bi, 0, 0, 0)),
            pl.BlockSpec((None, heads, 1, tk), lambda bi, kb: (bi, 0, 0, kb)),
            pl.BlockSpec((None, heads, 1, n_new), lambda bi, kb: (bi, 0, 0, 0)),
        ],
        out_specs=pl.BlockSpec((None, tq, hd), lambda bi, kb: (bi, 0, 0)),
        out_shape=jax.ShapeDtypeStruct((b, tq, hd), BF16),
        scratch_shapes=[pltpu.VMEM((heads, tq, LANES), F32), pltpu.VMEM((heads, tq, LANES), F32),
                        pltpu.VMEM((heads, tq, HEAD_DIM), F32)],
        compiler_params=_cparams(("parallel", "arbitrary")),
        name="fox_decode",
    )(q, k_new, v_new, k_cache, v_cache, cq, ck_cache, ck_new)


SB_DEAD = -160.0


def _suffix_sums(sp, ntri, carry):
    width = ntri.shape[0]
    n = sp.shape[1]
    parts = []
    for c0 in range(n - width, -1, -width):
        piece = sp[:, c0:c0 + width]
        hi = piece.astype(BF16)
        lo = (piece - hi.astype(F32)).astype(BF16)
        c = _dot(hi, ntri) + _dot(lo, ntri) + carry
        carry = c[:, 0:1]
        parts.append(c)
    parts.reverse()
    return (parts[0] if len(parts) == 1 else jnp.concatenate(parts, axis=1)), carry


def _mask_tail(x, mask):
    if mask is None:
        return x
    head = x.shape[1] - mask.shape[1]
    tail = jnp.where(mask, x[:, head:], 0.0)
    return tail if head == 0 else jnp.concatenate([x[:, :head], tail], axis=1)


def _sb_weights(z2, ntri, carry, mask):
    sp = _mask_tail(_softplus2(z2), mask)
    c, new_carry = _suffix_sums(sp, ntri, carry)
    a = _mask_tail(jnp.exp2(z2 + c), mask)
    return a.astype(BF16), new_carry


def _sb_prompt_kernel(q_ref, k_ref, v_ref, ntri_ref, o_ref, r_ref, acc_ref, *, tq, bs):
    qi = pl.program_id(1)
    nsub = tq // bs
    ntri = ntri_ref[...]
    row = lax.broadcasted_iota(jnp.int32, (bs, bs), 0)
    col = lax.broadcasted_iota(jnp.int32, (bs, bs), 1)

    def sub_rows(s):
        return slice(s * bs, (s + 1) * bs)

    def block(s, j, nblk, carry, mask):
        rows = pl.ds(pl.multiple_of(j * bs, bs), nblk * bs)
        a, carry = _sb_weights(_dot_nt(q_ref[sub_rows(s), :], k_ref[rows, :]), ntri, carry, mask)
        return _dot(a, v_ref[rows, :]), carry

    def first_units(at_origin):
        for s in range(nsub):
            if s == 0 and at_origin:
                out, carry = block(s, 0, 1, jnp.zeros((bs, 1), F32), col < row)
            else:
                out, carry = block(s, qi * nsub + s - 1, 2, jnp.zeros((bs, 1), F32), col < row)
            acc_ref[s] = out
            r_ref[s] = jnp.broadcast_to(carry, (bs, LANES))

    @pl.when(qi == 0)
    def _():
        first_units(True)

    @pl.when(qi > 0)
    def _():
        first_units(False)

    def live(s):
        return jnp.max(r_ref[s]) > SB_DEAD

    alive_after_first = [live(s) for s in range(nsub)]

    def rest(s):
        g = qi * nsub + s

        def step(j, nblk):
            out, carry = block(s, j, nblk, r_ref[s][:, 0:1], None)
            acc_ref[s] += out
            r_ref[s] = jnp.broadcast_to(carry, (bs, LANES))

        n_rest = jnp.maximum(g - 1, 0)

        def cond(state):
            i, alive = state
            return jnp.logical_and(i < n_rest // 4, alive)

        def body(state):
            i, _ = state
            step(n_rest - 4 - 4 * i, 4)
            return i + 1, live(s)

        _, alive = lax.while_loop(cond, body, (0, alive_after_first[s]))

        @pl.when(jnp.logical_and(n_rest % 4 >= 2, alive))
        def _():
            step(n_rest % 2, 2)

        @pl.when(jnp.logical_and(n_rest % 2 == 1, alive))
        def _():
            step(0, 1)

        o_ref[sub_rows(s), :] = acc_ref[s].astype(o_ref.dtype)

    for s in range(nsub):
        rest(s)


def sb_prompt(q, k, v, ntri, tq=2048, bs=256):
    t, hd = q.shape
    h = hd // HEAD_DIM
    tq = min(tq, t)
    bs = min(bs, tq)
    return pl.pallas_call(
        functools.partial(_sb_prompt_kernel, tq=tq, bs=bs),
        grid=(h, t // tq),
        in_specs=[
            pl.BlockSpec((tq, HEAD_DIM), lambda hi, qi: (qi, hi)),
            pl.BlockSpec((t, HEAD_DIM), lambda hi, qi: (0, hi)),
            pl.BlockSpec((t, HEAD_DIM), lambda hi, qi: (0, hi)),
            _resident(ntri.shape, lambda hi, qi: (0, 0)),
        ],
        out_specs=pl.BlockSpec((tq, HEAD_DIM), lambda hi, qi: (qi, hi)),
        out_shape=jax.ShapeDtypeStruct((t, hd), BF16),
        scratch_shapes=[pltpu.VMEM((tq // bs, bs, LANES), F32), pltpu.VMEM((tq // bs, bs, HEAD_DIM), F32)],
        compiler_params=_cparams(("parallel", "arbitrary")),
        name="sb_prompt",
    )(q, k, v, ntri)


def _sb_decode_kernel(q_ref, kn_ref, vn_ref, kc_hbm, vc_hbm, ntri_ref, o_ref, kbuf, vbuf, sem, r_ref, acc_ref,
                      *, heads, tk):
    b = pl.program_id(0)
    tq = q_ref.shape[0]
    n_new = kn_ref.shape[0]
    blk = tk * heads
    nkb = kc_hbm.shape[1] // blk

    def fetch(j, slot):
        rows = pl.ds(pl.multiple_of(j * blk, blk), blk)
        return (pltpu.make_async_copy(kc_hbm.at[b, rows, :], kbuf.at[slot], sem.at[0, slot]),
                pltpu.make_async_copy(vc_hbm.at[b, rows, :], vbuf.at[slot], sem.at[1, slot]))

    for copy in fetch(nkb - 1, 0):
        copy.start()

    def head_rows(h):
        return slice(h * tq, (h + 1) * tq)

    def stacked_scores(keys_of):
        return jnp.concatenate([_dot_nt(q_ref[:, _head_cols(h)], keys_of(h)) for h in range(heads)], axis=0)

    def live():
        return jnp.max(r_ref[...]) > SB_DEAD

    z2 = stacked_scores(lambda h: kn_ref[:, _head_cols(h)])
    row = lax.rem(lax.broadcasted_iota(jnp.int32, (heads * tq, n_new), 0), tq)
    col = lax.broadcasted_iota(jnp.int32, (heads * tq, n_new), 1)
    a, carry = _sb_weights(z2, ntri_ref[0:n_new, 0:n_new], jnp.zeros((heads * tq, 1), F32), col < row)
    for h in range(heads):
        acc_ref[head_rows(h), :] = _dot(a[head_rows(h), :], vn_ref[:, _head_cols(h)])
    r_ref[...] = jnp.broadcast_to(carry, r_ref.shape)

    def body(state):
        i, _ = state
        j = nkb - 1 - i
        slot = lax.rem(i, 2)
        for copy in fetch(j, slot):
            copy.wait()

        @pl.when(j > 0)
        def _():
            for copy in fetch(j - 1, 1 - slot):
                copy.start()

        z2 = stacked_scores(lambda h: _head_rows(kbuf.at[slot], h, heads, tk).astype(BF16))
        a, carry = _sb_weights(z2, ntri_ref[...], r_ref[:, 0:1], None)
        for h in range(heads):
            acc_ref[head_rows(h), :] += _dot(a[head_rows(h), :], _head_rows(vbuf.at[slot], h, heads, tk).astype(BF16))
        r_ref[...] = jnp.broadcast_to(carry, r_ref.shape)
        return i + 1, live()

    n_done, _ = lax.while_loop(lambda st: jnp.logical_and(st[0] < nkb, st[1]), body, (0, live()))

    @pl.when(n_done < nkb)
    def _():
        for copy in fetch(nkb - 1 - n_done, lax.rem(n_done, 2)):
            copy.wait()

    for h in range(heads):
        o_ref[:, _head_cols(h)] = acc_ref[head_rows(h), :].astype(o_ref.dtype)


def sb_decode(q, k_new, v_new, k_cache, v_cache, ntri, heads, tk=512):
    b, tq, hd = q.shape
    n_new = k_new.shape[1]
    p_len = k_cache.shape[1] // heads
    tk = min(tk, p_len)
    return pl.pallas_call(
        functools.partial(_sb_decode_kernel, heads=heads, tk=tk),
        grid=(b,),
        in_specs=[
            pl.BlockSpec((None, tq, hd), lambda bi: (bi, 0, 0)),
            pl.BlockSpec((None, n_new, hd), lambda bi: (bi, 0, 0)),
            pl.BlockSpec((None, n_new, hd), lambda bi: (bi, 0, 0)),
            pl.BlockSpec(memory_space=pl.ANY),
            pl.BlockSpec(memory_space=pl.ANY),
            _resident(ntri.shape, lambda bi: (0, 0)),
        ],
        out_specs=pl.BlockSpec((None, tq, hd), lambda bi: (bi, 0, 0)),
        out_shape=jax.ShapeDtypeStruct((b, tq, hd), BF16),
        scratch_shapes=[pltpu.VMEM((2, tk * heads, HEAD_DIM), F32), pltpu.VMEM((2, tk * heads, HEAD_DIM), F32),
                        pltpu.SemaphoreType.DMA((2, 2)),
                        pltpu.VMEM((heads * tq, LANES), F32), pltpu.VMEM((heads * tq, HEAD_DIM), F32)],
        compiler_params=_cparams(("arbitrary",)),
        name="sb_decode",
    )(q, k_new, v_new, k_cache, v_cache, ntri)


def _out_proj_kernel(xa_ref, xb_ref, w_ref, h_ref, o_ref):
    ka = xa_ref.shape[1]
    o_ref[...] = h_ref[...] + _dot(xa_ref[...], w_ref[:ka, :]) + _dot(xb_ref[...], w_ref[ka:, :])


def out_proj(xa, xb, col_b, w, h, tm=512):
    m, d = h.shape
    ka = w.shape[0] // 2
    tm = min(tm, m)
    return pl.pallas_call(
        _out_proj_kernel,
        grid=(m // tm,),
        in_specs=[
            pl.BlockSpec((tm, ka), lambda i: (i, 0)),
            pl.BlockSpec((tm, ka), lambda i: (i, col_b)),
            _resident(w.shape, lambda i: (0, 0)),
            pl.BlockSpec((tm, d), lambda i: (i, 0)),
        ],
        out_specs=pl.BlockSpec((tm, d), lambda i: (i, 0)),
        out_shape=jax.ShapeDtypeStruct((m, d), F32),
        compiler_params=_cparams(("parallel",)),
        name="out_proj",
    )(xa, xb, w, h)


def _ffn_kernel(h_ref, g_ref, wg_ref, wu_ref, wd_ref, gf_ref, o_ref, xn_ref, *, final_norm):
    fi = pl.program_id(1)

    @pl.when(fi == 0)
    def _():
        xn_ref[...] = _rms(h_ref[...], g_ref[...]).astype(BF16)
        o_ref[...] = jnp.zeros_like(o_ref)

    xn = xn_ref[...]
    gate = _dot(xn, wg_ref[...])
    up = _dot(xn, wu_ref[...])
    mid = (gate / (1.0 + jnp.exp(-gate)) * up).astype(BF16)
    width = wd_ref.shape[0]
    for c0 in range(0, o_ref.shape[1], width):
        o_ref[:, c0:c0 + width] += _dot(mid, wd_ref[:, c0:c0 + width])

    @pl.when(fi == pl.num_programs(1) - 1)
    def _():
        hn = h_ref[...] + o_ref[...]
        o_ref[...] = _rms(hn, gf_ref[...]) if final_norm else hn


def ffn(h, g, w_gate, w_up, w_down, layer, g_final, final_norm, tm=1024, tf=512):
    m, d = h.shape
    f = w_gate.shape[2]
    tm = min(tm, m)
    return pl.pallas_call(
        functools.partial(_ffn_kernel, final_norm=final_norm),
        grid=(m // tm, f // tf),
        in_specs=[
            pl.BlockSpec((tm, d), lambda i, j: (i, 0), pipeline_mode=pl.Buffered(1)),
            _resident((1, d), lambda i, j: (0, 0)),
            pl.BlockSpec((None, d, tf), lambda i, j: (layer, 0, j)),
            pl.BlockSpec((None, d, tf), lambda i, j: (layer, 0, j)),
            pl.BlockSpec((None, tf, d), lambda i, j: (layer, j, 0)),
            _resident((1, d), lambda i, j: (0, 0)),
        ],
        out_specs=pl.BlockSpec((tm, d), lambda i, j: (i, 0)),
        out_shape=jax.ShapeDtypeStruct((m, d), F32),
        scratch_shapes=[pltpu.VMEM((tm, d), BF16)],
        compiler_params=_cparams(("parallel", "arbitrary")),
        name="ffn",
    )(h, g, w_gate, w_up, w_down, g_final)


def _pad_rows(x, rows):
    return jnp.pad(x, ((0, 0), (0, rows - x.shape[1]), (0, 0)))


def _neg_tri_matrix():
    idx = np.arange(TRI)
    return jnp.asarray(-(idx[:, None] >= idx[None, :]).astype(np.float32), dtype=BF16)


def _layer0_groups(pool_w, fox_w):
    outs = [("f32",), ("qscaled",), ("f32", "bf16"), ("f32", "bf16")]
    groups, out_defs, col = [], [], 0
    for kinds in outs:
        width = pool_w if col == 0 else fox_w
        o = []
        for kind in kinds:
            o.append((len(out_defs), 0, kind))
            out_defs.append((width, F32 if kind == "f32" else BF16))
        groups.append((col, width, tuple(o)))
        col += width
    groups.append((None, LANES, ((len(out_defs), 0, "logsig"),)))
    out_defs.append((LANES, F32))
    return tuple(groups), out_defs


def _layer1_groups(d):
    half = d // 2
    groups = []
    for part, outs in enumerate([(0,), (1, 2), (3, 4)]):
        for c in range(2):
            kinds = ("qscaled",) if part == 0 else ("f32", "bf16")
            groups.append((part * d + c * half, half,
                           tuple((oi, c * half, kind) for oi, kind in zip(outs, kinds))))
    out_defs = [(d, BF16), (d, F32), (d, BF16), (d, F32), (d, BF16)]
    return tuple(groups), out_defs


def _run_trunk(x, n_past, pool_c, fk_c, fv_c, flf_c, sk_c, sv_c, wts):
    b, t, d = x.shape
    m = b * t
    h = x.reshape(m, d)
    pool_w = wts["w_pool"].shape[0] * wts["w_pool"].shape[1]
    fox_w = d - pool_w
    fox_h = fox_w // HEAD_DIM
    sb_h = d // HEAD_DIM
    first = fk_c is None
    zero_bias = jnp.zeros((1, LANES), F32)

    groups, out_defs = _layer0_groups(pool_w, fox_w)
    u, q, k, kb, v, vb, lf = norm_proj(h, wts["ln_mix0"], wts["w_in"], wts["w_forget"], wts["b_forget"], groups,
                                        out_defs, tm=256)
    log_f = lf[:, :fox_h]
    if first:
        prefix = jnp.zeros((b, POOL_HALO, pool_w), F32)
    else:
        prefix = jnp.pad(pool_c, ((0, 0), (POOL_HALO - POOL_STATE, 0), (0, 0)))
    y_pool = pool_mixer(u.reshape(b, t, pool_w), prefix, wts["w_pool"], wts["pool_scale"], n_past)
    if t >= POOL_STATE:
        pool_state = u.reshape(b, t, pool_w)[:, t - POOL_STATE:]
    else:
        pool_state = jnp.concatenate([prefix[:, 1:], u.reshape(b, t, pool_w)], axis=1)[:, -POOL_STATE:]

    if first:
        bs = min(512, t)
        tq = min(4 * bs, t)
        cum2 = cumsum_rows(lf)[:, :fox_h].T * LOG2E
        ck = cum2.reshape(fox_h, t // bs, 1, bs)
        o = fox_prompt(q, kb, vb, ck, ck[:, :, 0, bs - 1], tq, bs)
    else:
        n_new = LANES
        lf_new = log_f.reshape(b, t, fox_h).transpose(1, 0, 2).reshape(t, b * fox_h)
        lf_all = jnp.concatenate([flf_c.transpose(1, 0, 2).reshape(n_past, b * fox_h), lf_new], axis=0)
        rows = n_past + t
        rows_pad = -(-rows // 512) * 512
        lf_all = jnp.pad(lf_all, ((0, rows_pad - rows), (0, LANES - b * fox_h)))
        cum2 = cumsum_rows(lf_all)[:rows, :b * fox_h].reshape(rows, b, fox_h).transpose(1, 2, 0) * LOG2E
        ck_cache = cum2[:, :, None, :n_past]
        cum_new = cum2[:, :, n_past:]
        ck_new = jnp.pad(cum_new, ((0, 0), (0, 0), (0, n_new - t)))[:, :, None, :]
        cq = jnp.broadcast_to(cum_new[:, :, :, None], (b, fox_h, t, LANES))
        o = fox_decode(q.reshape(b, t, fox_w), _pad_rows(kb.reshape(b, t, fox_w), n_new),
                       _pad_rows(vb.reshape(b, t, fox_w), n_new), fk_c, fv_c, cq, ck_cache, ck_new, fox_h)
        o = o.reshape(m, fox_w)
    h = out_proj(y_pool.reshape(m, pool_w), o, 0, wts["w_out_ab"], h)
    h = ffn(h, wts["ln_ffn0"], wts["w_gate"], wts["w_up"], wts["w_down"], 0, wts["ln_final"], False)

    groups1, out_defs1 = _layer1_groups(d)
    q1, k1, k1b, v1, v1b = norm_proj(h, wts["ln_mix1"], wts["w_qkv"], wts["w_forget"], zero_bias, groups1, out_defs1,
                                     tm=256)
    ntri = _neg_tri_matrix()
    if first:
        o1 = sb_prompt(q1, k1b, v1b, ntri)
    else:
        n_new = LANES
        o1 = sb_decode(q1.reshape(b, t, d), _pad_rows(k1b.reshape(b, t, d), n_new),
                       _pad_rows(v1b.reshape(b, t, d), n_new), sk_c, sv_c, ntri, sb_h).reshape(m, d)
    h = out_proj(o1, o1, 1, wts["w_out_sb"], h)
    y = ffn(h, wts["ln_ffn1"], wts["w_gate"], wts["w_up"], wts["w_down"], 1, wts["ln_final"], True)

    return (y.reshape(b, t, d), pool_state[None], k.reshape(1, b, t, fox_h, HEAD_DIM),
            v.reshape(1, b, t, fox_h, HEAD_DIM), log_f.reshape(1, b, t, fox_h),
            k1.reshape(1, b, t, sb_h, HEAD_DIM), v1.reshape(1, b, t, sb_h, HEAD_DIM))


def kernel(x_prompt, x_sample, cache_pool, cache_fox_k, cache_fox_v, cache_fox_logf, cache_sb_k, cache_sb_v, ln_mix, w_in_ab, b_forget, w_pool, pool_scale, w_out_ab, w_qkv_sb, w_out_sb, ln_ffn, w_gate, w_up, w_down, ln_final):
    fox_h = b_forget.shape[-1]
    n_main = w_in_ab.shape[-1] - fox_h
    wts = {
        "ln_mix0": ln_mix[0][None], "ln_mix1": ln_mix[1][None],
        "ln_ffn0": ln_ffn[0][None], "ln_ffn1": ln_ffn[1][None], "ln_final": ln_final[None],
        "w_in": w_in_ab[0, :, :n_main].astype(BF16),
        "w_forget": jnp.pad(w_in_ab[0, :, n_main:], ((0, 0), (0, LANES - fox_h))).astype(BF16),
        "b_forget": jnp.pad(b_forget[0], (0, LANES - fox_h))[None],
        "w_pool": w_pool[0].astype(BF16), "pool_scale": pool_scale[0][None],
        "w_out_ab": w_out_ab[0].astype(BF16), "w_qkv": w_qkv_sb[0].astype(BF16),
        "w_out_sb": w_out_sb[0].astype(BF16),
        "w_gate": w_gate.astype(BF16), "w_up": w_up.astype(BF16), "w_down": w_down.astype(BF16),
    }
    out_p = _run_trunk(x_prompt, 0, None, None, None, None, None, None, wts)
    n_past = cache_fox_k.shape[2]
    bs = x_sample.shape[0]

    def rows_of(cache):
        return cache.reshape(bs, -1, HEAD_DIM)

    out_s = _run_trunk(x_sample, n_past, cache_pool[0], rows_of(cache_fox_k), rows_of(cache_fox_v),
                       cache_fox_logf[0], rows_of(cache_sb_k), rows_of(cache_sb_v), wts)
    return (out_p[0], out_s[0]) + out_p[1:] + out_s[1:]
```

```python
import functools

import numpy as np
import jax
import jax.numpy as jnp
from jax import lax
from jax.experimental import pallas as pl
from jax.experimental.pallas import tpu as pltpu

HEAD_DIM = 128
EPS = 1e-6
POOL_WINDOWS = (2, 4, 8, 16)
POOL_STATE = max(POOL_WINDOWS) - 1
POOL_HALO = 16
LANES = 128
TRI = 256
VMEM_LIMIT = 56 * 1024 * 1024
LOG2E = 1.4426950408889634
QK_SCALE2 = HEAD_DIM ** -0.5 * LOG2E

F32 = jnp.float32
BF16 = jnp.bfloat16


def _cparams(sem, vmem=VMEM_LIMIT):
    return pltpu.CompilerParams(dimension_semantics=sem, vmem_limit_bytes=vmem)


def _resident(shape, index_map):
    return pl.BlockSpec(shape, index_map, pipeline_mode=pl.Buffered(1))


def _dot(a, b):
    return jnp.dot(a, b, preferred_element_type=F32)


def _dot_nt(a, b):
    return lax.dot_general(a, b, (((1,), (1,)), ((), ())), preferred_element_type=F32)


def _rms(x, g):
    ms = jnp.mean(x * x, axis=-1, keepdims=True)
    return x * lax.rsqrt(ms + EPS) * g


def _softplus(z):
    return jnp.maximum(z, 0.0) + jnp.log1p(jnp.exp(-jnp.abs(z)))


def _softplus2(z2):
    return jnp.maximum(z2, 0.0) + jnp.log(1.0 + jnp.exp2(-jnp.abs(z2))) * LOG2E


def _tile_lanes(x, n):
    reps = n // LANES
    return x if reps == 1 else jnp.concatenate([x] * reps, axis=1)


def _head_cols(h):
    return slice(h * HEAD_DIM, (h + 1) * HEAD_DIM)


def _head_rows(ref, h, heads, n):
    return ref[pl.ds(h, n, stride=heads), :]


def _norm_proj_kernel(x_ref, g_ref, w_ref, w2_ref, b_ref, *out_refs, groups):
    xn = _rms(x_ref[...], g_ref[...]).astype(BF16)
    for w_col, width, outs in groups:
        acc = _dot(xn, w2_ref[...] if w_col is None else w_ref[:, w_col:w_col + width])
        for out_idx, out_col, kind in outs:
            if kind == "logsig":
                val = -_softplus(-(acc + b_ref[...]))
            elif kind == "qscaled":
                val = acc * QK_SCALE2
            else:
                val = acc
            o = out_refs[out_idx]
            o[:, out_col:out_col + width] = val.astype(o.dtype)


def norm_proj(x, g, w, w2, bias, groups, out_defs, tm):
    m, d = x.shape
    n = w.shape[1]
    tm = min(tm, m)
    out_shape = [jax.ShapeDtypeStruct((m, c), dt) for c, dt in out_defs]
    out_specs = [pl.BlockSpec((tm, c), lambda i: (i, 0)) for c, _ in out_defs]
    return pl.pallas_call(
        functools.partial(_norm_proj_kernel, groups=groups),
        grid=(m // tm,),
        in_specs=[
            pl.BlockSpec((tm, d), lambda i: (i, 0)),
            _resident((1, d), lambda i: (0, 0)),
            _resident((d, n), lambda i: (0, 0)),
            _resident((d, LANES), lambda i: (0, 0)),
            _resident((1, LANES), lambda i: (0, 0)),
        ],
        out_specs=out_specs,
        out_shape=out_shape,
        compiler_params=_cparams(("parallel",)),
        name="norm_proj",
    )(x, g, w, w2, bias)


def _cumsum_kernel(x_ref, o_ref, carry_ref, *, tb):
    @pl.when(pl.program_id(0) == 0)
    def _():
        carry_ref[...] = jnp.zeros_like(carry_ref)

    x = x_ref[...]
    row = lax.broadcasted_iota(jnp.int32, (tb, tb), 0)
    col = lax.broadcasted_iota(jnp.int32, (tb, tb), 1)
    low = jnp.where(col <= row, 1.0, 0.0).astype(BF16)
    hi = x.astype(BF16)
    r1 = x - hi.astype(F32)
    mid = r1.astype(BF16)
    lo = (r1 - mid.astype(F32)).astype(BF16)
    out = _dot(low, hi) + _dot(low, mid) + _dot(low, lo) + carry_ref[0:1, :]
    o_ref[...] = out
    carry_ref[...] = jnp.broadcast_to(out[tb - 1:tb, :], carry_ref.shape)


def cumsum_rows(x, tb=512):
    t, c = x.shape
    tb = min(tb, t)
    return pl.pallas_call(
        functools.partial(_cumsum_kernel, tb=tb),
        grid=(t // tb,),
        in_specs=[pl.BlockSpec((tb, c), lambda i: (i, 0))],
        out_specs=pl.BlockSpec((tb, c), lambda i: (i, 0)),
        out_shape=jax.ShapeDtypeStruct((t, c), F32),
        scratch_shapes=[pltpu.VMEM((8, c), F32)],
        compiler_params=_cparams(("arbitrary",)),
        name="cumsum_rows",
    )(x)


def _pool_kernel(u_ref, pre_ref, w_ref, sc_ref, o_ref, buf_ref, *, tm, pos0, group):
    mi = pl.program_id(1)

    @pl.when(mi == 0)
    def _():
        buf_ref[0:POOL_HALO, :] = pre_ref[...]

    @pl.when(mi > 0)
    def _():
        buf_ref[0:POOL_HALO, :] = buf_ref[tm:tm + POOL_HALO, :]

    buf_ref[POOL_HALO:POOL_HALO + tm, :] = u_ref[...]
    pos = pos0 + mi * tm + lax.broadcasted_iota(jnp.int32, (tm, 1), 0)
    for gi, win in enumerate(POOL_WINDOWS):
        cs = slice(gi * group, (gi + 1) * group)
        tot = buf_ref[POOL_HALO:POOL_HALO + tm, cs]
        for back in range(1, win):
            tot = tot + buf_ref[POOL_HALO - back:POOL_HALO - back + tm, cs]
        count = jnp.minimum(pos + 1, win).astype(F32)
        diff = tot / count - u_ref[:, cs]
        y = _dot(diff.astype(BF16), w_ref[gi]) * sc_ref[:, cs]
        o_ref[:, cs] = y.astype(o_ref.dtype)


def pool_mixer(u, prefix, w_pool, scale, pos0, tm=512):
    b, t, c = u.shape
    tm = min(tm, t)
    group = c // len(POOL_WINDOWS)
    return pl.pallas_call(
        functools.partial(_pool_kernel, tm=tm, pos0=pos0, group=group),
        grid=(b, t // tm),
        in_specs=[
            pl.BlockSpec((None, tm, c), lambda bi, mi: (bi, mi, 0)),
            pl.BlockSpec((None, POOL_HALO, c), lambda bi, mi: (bi, 0, 0)),
            _resident(w_pool.shape, lambda bi, mi: (0, 0, 0)),
            _resident((1, c), lambda bi, mi: (0, 0)),
        ],
        out_specs=pl.BlockSpec((None, tm, c), lambda bi, mi: (bi, mi, 0)),
        out_shape=jax.ShapeDtypeStruct((b, t, c), BF16),
        scratch_shapes=[pltpu.VMEM((POOL_HALO + tm, c), F32)],
        compiler_params=_cparams(("arbitrary", "arbitrary")),
        name="pool_mixer",
    )(u, prefix, w_pool, scale)


def _fox_update(s2, v, m_prev, l_prev, acc_prev):
    m_new = jnp.maximum(m_prev, jnp.max(s2, axis=1, keepdims=True))
    alpha = jnp.exp2(m_prev - m_new)
    p = jnp.exp2(s2 - _tile_lanes(m_new, s2.shape[1]))
    l_new = alpha * l_prev + jnp.sum(p, axis=1, keepdims=True)
    acc_new = alpha * acc_prev + _dot(p.astype(BF16), v)
    return m_new, l_new, acc_new


FOX_DEAD = -160.0


def _fox_prompt_kernel(ck_end_ref, q_ref, k_ref, v_ref, ck_ref, o_ref, m_ref, l_ref, acc_ref, cq_ref, kmax_ref,
                       *, tq, bs):
    hi = pl.program_id(0)
    qi = pl.program_id(1)
    nq = pl.num_programs(1)
    nsub = tq // bs

    @pl.when(qi == 0)
    def _():
        def norm_body(j, best):
            kb = k_ref[pl.ds(pl.multiple_of(j * tq, tq), tq), :].astype(F32)
            return jnp.maximum(best, jnp.sum(kb * kb, axis=1, keepdims=True))

        best = lax.fori_loop(0, nq, norm_body, jnp.zeros((tq, 1), F32))
        kmax_ref[...] = jnp.broadcast_to(jnp.sqrt(jnp.max(best, axis=0, keepdims=True)), kmax_ref.shape)

    row = lax.broadcasted_iota(jnp.int32, (bs, bs), 0)
    col = lax.broadcasted_iota(jnp.int32, (bs, bs), 1)

    def sub_rows(s):
        return slice(s * bs, (s + 1) * bs)

    for s in range(nsub):
        own = jnp.where(row == col, jnp.broadcast_to(ck_ref[qi * nsub + s], (bs, bs)), 0.0)
        cq_ref[sub_rows(s), :] = jnp.broadcast_to(jnp.sum(own, axis=1, keepdims=True), (bs, LANES))

    def update(s, j, nblk, causal, fresh=False):
        width = nblk * bs
        rows = pl.ds(pl.multiple_of(j * bs, bs), width)
        cq = cq_ref[sub_rows(s), :]
        ck = ck_ref[j] if nblk == 1 else jnp.concatenate([ck_ref[j + i] for i in range(nblk)], axis=1)
        s2 = _dot_nt(q_ref[sub_rows(s), :], k_ref[rows, :]) + _tile_lanes(cq, width) - ck
        if causal:
            tail = jnp.where(col <= row, s2[:, width - bs:], -jnp.inf)
            s2 = tail if nblk == 1 else jnp.concatenate([s2[:, :width - bs], tail], axis=1)
        if fresh:
            state = (jnp.full((bs, LANES), -jnp.inf, F32), jnp.zeros((bs, LANES), F32), jnp.zeros((bs, HEAD_DIM), F32))
        else:
            state = (m_ref[s], l_ref[s], acc_ref[s])
        m_ref[s], l_ref[s], acc_ref[s] = _fox_update(s2, v_ref[rows, :], *state)

    def first_units(at_origin):
        for s in range(nsub):
            if s == 0 and at_origin:
                update(s, 0, 1, True, fresh=True)
            else:
                update(s, qi * nsub + s - 1, 2, True, fresh=True)

    @pl.when(qi == 0)
    def _():
        first_units(True)

    @pl.when(qi > 0)
    def _():
        first_units(False)

    def rest(s):
        g = qi * nsub + s
        qf = q_ref[sub_rows(s), :].astype(F32)
        reach = (jnp.sqrt(jnp.sum(qf * qf, axis=1, keepdims=True)) * kmax_ref[0:1, 0:1]
                 + cq_ref[sub_rows(s), 0:1])

        def margin():
            return jnp.max(reach - m_ref[s][:, 0:1])

        def live(gap, j_last):
            return gap - ck_end_ref[hi, jnp.maximum(j_last, 0)] > FOX_DEAD

        n_rest = jnp.maximum(g - 1, 0)
        gap0 = margin()
        pair = n_rest >= 2

        @pl.when(jnp.logical_and(pair, live(gap0, n_rest - 1)))
        def _():
            update(s, n_rest - 2, 2, False)

        n_left = jnp.where(pair, n_rest - 2, n_rest)

        def cond(state):
            i, alive = state
            return jnp.logical_and(i < n_left // 4, alive)

        def body(state):
            i, _ = state
            update(s, n_left - 4 - 4 * i, 4, False)
            return i + 1, live(margin(), n_left - 5 - 4 * i)

        _, alive = lax.while_loop(cond, body, (0, live(gap0, n_left - 1)))
        rem = n_left % 4

        @pl.when(jnp.logical_and(rem >= 2, alive))
        def _():
            update(s, rem % 2, 2, False)

        @pl.when(jnp.logical_and(rem % 2 == 1, alive))
        def _():
            update(s, 0, 1, False)

        o_ref[sub_rows(s), :] = (acc_ref[s] / l_ref[s]).astype(o_ref.dtype)

    for s in range(nsub):
        rest(s)


def fox_prompt(q, k, v, ck, ck_end, tq, bs):
    t, hd = q.shape
    h = hd // HEAD_DIM
    nq = t // tq
    return pl.pallas_call(
        functools.partial(_fox_prompt_kernel, tq=tq, bs=bs),
        grid=(h, nq),
        in_specs=[
            pl.BlockSpec(memory_space=pltpu.SMEM),
            pl.BlockSpec((tq, HEAD_DIM), lambda hi, qi: (qi, hi)),
            pl.BlockSpec((t, HEAD_DIM), lambda hi, qi: (0, hi)),
            pl.BlockSpec((t, HEAD_DIM), lambda hi, qi: (0, hi)),
            pl.BlockSpec((None, t // bs, 1, bs), lambda hi, qi: (hi, 0, 0, 0)),
        ],
        out_specs=pl.BlockSpec((tq, HEAD_DIM), lambda hi, qi: (qi, hi)),
        out_shape=jax.ShapeDtypeStruct((t, hd), BF16),
        scratch_shapes=[pltpu.VMEM((tq // bs, bs, LANES), F32), pltpu.VMEM((tq // bs, bs, LANES), F32),
                        pltpu.VMEM((tq // bs, bs, HEAD_DIM), F32), pltpu.VMEM((tq, LANES), F32),
                        pltpu.VMEM((8, LANES), F32)],
        compiler_params=_cparams(("arbitrary", "arbitrary")),
        name="fox_prompt",
    )(ck_end, q, k, v, ck)


def _fox_decode_kernel(q_ref, kn_ref, vn_ref, kc_ref, vc_ref, cq_ref, ckc_ref, ckn_ref, o_ref,
                       m_ref, l_ref, acc_ref, *, heads, tk):
    kb = pl.program_id(1)
    tq = q_ref.shape[0]
    n_new = kn_ref.shape[0]

    @pl.when(kb == 0)
    def _():
        row = lax.broadcasted_iota(jnp.int32, (tq, n_new), 0)
        col = lax.broadcasted_iota(jnp.int32, (tq, n_new), 1)
        for h in range(heads):
            hc = _head_cols(h)
            s2 = _dot_nt(q_ref[:, hc], kn_ref[:, hc]) + _tile_lanes(cq_ref[h], n_new) - ckn_ref[h]
            s2 = jnp.where(col <= row, s2, -jnp.inf)
            m = jnp.max(s2, axis=1, keepdims=True)
            p = jnp.exp2(s2 - m)
            m_ref[h] = jnp.broadcast_to(m, (tq, LANES))
            l_ref[h] = jnp.broadcast_to(jnp.sum(p, axis=1, keepdims=True), (tq, LANES))
            acc_ref[h] = _dot(p.astype(BF16), vn_ref[:, hc])

    for h in range(heads):
        k_h = _head_rows(kc_ref, h, heads, tk).astype(BF16)
        v_h = _head_rows(vc_ref, h, heads, tk).astype(BF16)
        s2 = _dot_nt(q_ref[:, _head_cols(h)], k_h) + _tile_lanes(cq_ref[h], tk) - ckc_ref[h]
        m_ref[h], l_ref[h], acc_ref[h] = _fox_update(s2, v_h, m_ref[h], l_ref[h], acc_ref[h])

    @pl.when(kb == pl.num_programs(1) - 1)
    def _():
        for h in range(heads):
            o_ref[:, _head_cols(h)] = (acc_ref[h] / l_ref[h]).astype(o_ref.dtype)


def fox_decode(q, k_new, v_new, k_cache, v_cache, cq, ck_cache, ck_new, heads, tk=2048):
    b, tq, hd = q.shape
    n_new = k_new.shape[1]
    p_len = k_cache.shape[1] // heads
    tk = min(tk, p_len)
    return pl.pallas_call(
        functools.partial(_fox_decode_kernel, heads=heads, tk=tk),
        grid=(b, p_len // tk),
        in_specs=[
            pl.BlockSpec((None, tq, hd), lambda bi, kb: (bi, 0, 0)),
            pl.BlockSpec((None, n_new, hd), lambda bi, kb: (bi, 0, 0)),
            pl.BlockSpec((None, n_new, hd), lambda bi, kb: (bi, 0, 0)),
            pl.BlockSpec((None, tk * heads, HEAD_DIM), lambda bi, kb: (bi, kb, 0)),
            pl.BlockSpec((None, tk * heads, HEAD_DIM), lambda bi, kb: (bi, kb, 0)),
            pl.BlockSpec((None, heads, tq, LANES), lambda bi, kb: (bi, 0, 0, 0)),
            pl.BlockSpec((None, heads, 1, tk), lambda bi, kb: (bi, 0, 0, kb)),
            pl.BlockSpec((None, heads, 1, n_new), lambda bi, kb: (bi, 0, 0, 0)),
        ],
        out_specs=pl.BlockSpec((None, tq, hd), lambda bi, kb: (bi, 0, 0)),
        out_shape=jax.ShapeDtypeStruct((b, tq, hd), BF16),
        scratch_shapes=[pltpu.VMEM((heads, tq, LANES), F32), pltpu.VMEM((heads, tq, LANES), F32),
                        pltpu.VMEM((heads, tq, HEAD_DIM), F32)],
        compiler_params=_cparams(("parallel", "arbitrary")),
        name="fox_decode",
    )(q, k_new, v_new, k_cache, v_cache, cq, ck_cache, ck_new)


SB_DEAD = -160.0


def _suffix_sums(sp, ntri, carry):
    width = ntri.shape[0]
    n = sp.shape[1]
    parts = []
    for c0 in range(n - width, -1, -width):
        piece = sp[:, c0:c0 + width]
        hi = piece.astype(BF16)
        lo = (piece - hi.astype(F32)).astype(BF16)
        c = _dot(hi, ntri) + _dot(lo, ntri) + carry
        carry = c[:, 0:1]
        parts.append(c)
    parts.reverse()
    return (parts[0] if len(parts) == 1 else jnp.concatenate(parts, axis=1)), carry


def _mask_tail(x, mask):
    if mask is None:
        return x
    head = x.shape[1] - mask.shape[1]
    tail = jnp.where(mask, x[:, head:], 0.0)
    return tail if head == 0 else jnp.concatenate([x[:, :head], tail], axis=1)


def _sb_weights(z2, ntri, carry, mask):
    sp = _mask_tail(_softplus2(z2), mask)
    c, new_carry = _suffix_sums(sp, ntri, carry)
    a = _mask_tail(jnp.exp2(z2 + c), mask)
    return a.astype(BF16), new_carry


def _sb_prompt_kernel(q_ref, k_ref, v_ref, ntri_ref, o_ref, r_ref, acc_ref, *, tq, bs):
    qi = pl.program_id(1)
    nsub = tq // bs
    ntri = ntri_ref[...]
    row = lax.broadcasted_iota(jnp.int32, (bs, bs), 0)
    col = lax.broadcasted_iota(jnp.int32, (bs, bs), 1)

    def sub_rows(s):
        return slice(s * bs, (s + 1) * bs)

    def block(s, j, nblk, carry, mask):
        rows = pl.ds(pl.multiple_of(j * bs, bs), nblk * bs)
        a, carry = _sb_weights(_dot_nt(q_ref[sub_rows(s), :], k_ref[rows, :]), ntri, carry, mask)
        return _dot(a, v_ref[rows, :]), carry

    def first_units(at_origin):
        for s in range(nsub):
            if s == 0 and at_origin:
                out, carry = block(s, 0, 1, jnp.zeros((bs, 1), F32), col < row)
            else:
                out, carry = block(s, qi * nsub + s - 1, 2, jnp.zeros((bs, 1), F32), col < row)
            acc_ref[s] = out
            r_ref[s] = jnp.broadcast_to(carry, (bs, LANES))

    @pl.when(qi == 0)
    def _():
        first_units(True)

    @pl.when(qi > 0)
    def _():
        first_units(False)

    def live(s):
        return jnp.max(r_ref[s]) > SB_DEAD

    alive_after_first = [live(s) for s in range(nsub)]

    def rest(s):
        g = qi * nsub + s

        def step(j, nblk):
            out, carry = block(s, j, nblk, r_ref[s][:, 0:1], None)
            acc_ref[s] += out
            r_ref[s] = jnp.broadcast_to(carry, (bs, LANES))

        n_rest = jnp.maximum(g - 1, 0)

        def cond(state):
            i, alive = state
            return jnp.logical_and(i < n_rest // 4, alive)

        def body(state):
            i, _ = state
            step(n_rest - 4 - 4 * i, 4)
            return i + 1, live(s)

        _, alive = lax.while_loop(cond, body, (0, alive_after_first[s]))

        @pl.when(jnp.logical_and(n_rest % 4 >= 2, alive))
        def _():
            step(n_rest % 2, 2)

        @pl.when(jnp.logical_and(n_rest % 2 == 1, alive))
        def _():
            step(0, 1)

        o_ref[sub_rows(s), :] = acc_ref[s].astype(o_ref.dtype)

    for s in range(nsub):
        rest(s)


def sb_prompt(q, k, v, ntri, tq=2048, bs=256):
    t, hd = q.shape
    h = hd // HEAD_DIM
    tq = min(tq, t)
    bs = min(bs, tq)
    return pl.pallas_call(
        functools.partial(_sb_prompt_kernel, tq=tq, bs=bs),
        grid=(h, t // tq),
        in_specs=[
            pl.BlockSpec((tq, HEAD_DIM), lambda hi, qi: (qi, hi)),
            pl.BlockSpec((t, HEAD_DIM), lambda hi, qi: (0, hi)),
            pl.BlockSpec((t, HEAD_DIM), lambda hi, qi: (0, hi)),
            _resident(ntri.shape, lambda hi, qi: (0, 0)),
        ],
        out_specs=pl.BlockSpec((tq, HEAD_DIM), lambda hi, qi: (qi, hi)),
        out_shape=jax.ShapeDtypeStruct((t, hd), BF16),
        scratch_shapes=[pltpu.VMEM((tq // bs, bs, LANES), F32), pltpu.VMEM((tq // bs, bs, HEAD_DIM), F32)],
        compiler_params=_cparams(("parallel", "arbitrary")),
        name="sb_prompt",
    )(q, k, v, ntri)


def _sb_decode_kernel(q_ref, kn_ref, vn_ref, kc_hbm, vc_hbm, ntri_ref, o_ref, kbuf, vbuf, sem, r_ref, acc_ref,
                      *, heads, tk):
    b = pl.program_id(0)
    tq = q_ref.shape[0]
    n_new = kn_ref.shape[0]
    blk = tk * heads
    nkb = kc_hbm.shape[1] // blk

    def fetch(j, slot):
        rows = pl.ds(pl.multiple_of(j * blk, blk), blk)
        return (pltpu.make_async_copy(kc_hbm.at[b, rows, :], kbuf.at[slot], sem.at[0, slot]),
                pltpu.make_async_copy(vc_hbm.at[b, rows, :], vbuf.at[slot], sem.at[1, slot]))

    for copy in fetch(nkb - 1, 0):
        copy.start()

    def head_rows(h):
        return slice(h * tq, (h + 1) * tq)

    def stacked_scores(keys_of):
        return jnp.concatenate([_dot_nt(q_ref[:, _head_cols(h)], keys_of(h)) for h in range(heads)], axis=0)

    def live():
        return jnp.max(r_ref[...]) > SB_DEAD

    z2 = stacked_scores(lambda h: kn_ref[:, _head_cols(h)])
    row = lax.rem(lax.broadcasted_iota(jnp.int32, (heads * tq, n_new), 0), tq)
    col = lax.broadcasted_iota(jnp.int32, (heads * tq, n_new), 1)
    a, carry = _sb_weights(z2, ntri_ref[0:n_new, 0:n_new], jnp.zeros((heads * tq, 1), F32), col < row)
    for h in range(heads):
        acc_ref[head_rows(h), :] = _dot(a[head_rows(h), :], vn_ref[:, _head_cols(h)])
    r_ref[...] = jnp.broadcast_to(carry, r_ref.shape)

    def body(state):
        i, _ = state
        j = nkb - 1 - i
        slot = lax.rem(i, 2)
        for copy in fetch(j, slot):
            copy.wait()

        @pl.when(j > 0)
        def _():
            for copy in fetch(j - 1, 1 - slot):
                copy.start()

        z2 = stacked_scores(lambda h: _head_rows(kbuf.at[slot], h, heads, tk).astype(BF16))
        a, carry = _sb_weights(z2, ntri_ref[...], r_ref[:, 0:1], None)
        for h in range(heads):
            acc_ref[head_rows(h), :] += _dot(a[head_rows(h), :], _head_rows(vbuf.at[slot], h, heads, tk).astype(BF16))
        r_ref[...] = jnp.broadcast_to(carry, r_ref.shape)
        return i + 1, live()

    n_done, _ = lax.while_loop(lambda st: jnp.logical_and(st[0] < nkb, st[1]), body, (0, live()))

    @pl.when(n_done < nkb)
    def _():
        for copy in fetch(nkb - 1 - n_done, lax.rem(n_done, 2)):
            copy.wait()

    for h in range(heads):
        o_ref[:, _head_cols(h)] = acc_ref[head_rows(h), :].astype(o_ref.dtype)


def sb_decode(q, k_new, v_new, k_cache, v_cache, ntri, heads, tk=256):
    b, tq, hd = q.shape
    n_new = k_new.shape[1]
    p_len = k_cache.shape[1] // heads
    tk = min(tk, p_len)
    return pl.pallas_call(
        functools.partial(_sb_decode_kernel, heads=heads, tk=tk),
        grid=(b,),
        in_specs=[
            pl.BlockSpec((None, tq, hd), lambda bi: (bi, 0, 0)),
            pl.BlockSpec((None, n_new, hd), lambda bi: (bi, 0, 0)),
            pl.BlockSpec((None, n_new, hd), lambda bi: (bi, 0, 0)),
            pl.BlockSpec(memory_space=pl.ANY),
            pl.BlockSpec(memory_space=pl.ANY),
            _resident(ntri.shape, lambda bi: (0, 0)),
        ],
        out_specs=pl.BlockSpec((None, tq, hd), lambda bi: (bi, 0, 0)),
        out_shape=jax.ShapeDtypeStruct((b, tq, hd), BF16),
        scratch_shapes=[pltpu.VMEM((2, tk * heads, HEAD_DIM), F32), pltpu.VMEM((2, tk * heads, HEAD_DIM), F32),
                        pltpu.SemaphoreType.DMA((2, 2)),
                        pltpu.VMEM((heads * tq, LANES), F32), pltpu.VMEM((heads * tq, HEAD_DIM), F32)],
        compiler_params=_cparams(("arbitrary",)),
        name="sb_decode",
    )(q, k_new, v_new, k_cache, v_cache, ntri)


def _out_proj_kernel(xa_ref, xb_ref, w_ref, h_ref, o_ref):
    ka = xa_ref.shape[1]
    o_ref[...] = h_ref[...] + _dot(xa_ref[...], w_ref[:ka, :]) + _dot(xb_ref[...], w_ref[ka:, :])


def out_proj(xa, xb, col_b, w, h, tm=512):
    m, d = h.shape
    ka = w.shape[0] // 2
    tm = min(tm, m)
    return pl.pallas_call(
        _out_proj_kernel,
        grid=(m // tm,),
        in_specs=[
            pl.BlockSpec((tm, ka), lambda i: (i, 0)),
            pl.BlockSpec((tm, ka), lambda i: (i, col_b)),
            _resident(w.shape, lambda i: (0, 0)),
            pl.BlockSpec((tm, d), lambda i: (i, 0)),
        ],
        out_specs=pl.BlockSpec((tm, d), lambda i: (i, 0)),
        out_shape=jax.ShapeDtypeStruct((m, d), F32),
        compiler_params=_cparams(("parallel",)),
        name="out_proj",
    )(xa, xb, w, h)


def _ffn_kernel(h_ref, g_ref, wg_ref, wu_ref, wd_ref, gf_ref, o_ref, xn_ref, *, final_norm):
    fi = pl.program_id(1)

    @pl.when(fi == 0)
    def _():
        xn_ref[...] = _rms(h_ref[...], g_ref[...]).astype(BF16)
        o_ref[...] = jnp.zeros_like(o_ref)

    xn = xn_ref[...]
    gate = _dot(xn, wg_ref[...])
    up = _dot(xn, wu_ref[...])
    mid = (gate / (1.0 + jnp.exp(-gate)) * up).astype(BF16)
    width = wd_ref.shape[0]
    for c0 in range(0, o_ref.shape[1], width):
        o_ref[:, c0:c0 + width] += _dot(mid, wd_ref[:, c0:c0 + width])

    @pl.when(fi == pl.num_programs(1) - 1)
    def _():
        hn = h_ref[...] + o_ref[...]
        o_ref[...] = _rms(hn, gf_ref[...]) if final_norm else hn


def ffn(h, g, w_gate, w_up, w_down, layer, g_final, final_norm, tm=1024, tf=512):
    m, d = h.shape
    f = w_gate.shape[2]
    tm = min(tm, m)
    return pl.pallas_call(
        functools.partial(_ffn_kernel, final_norm=final_norm),
        grid=(m // tm, f // tf),
        in_specs=[
            pl.BlockSpec((tm, d), lambda i, j: (i, 0), pipeline_mode=pl.Buffered(1)),
            _resident((1, d), lambda i, j: (0, 0)),
            pl.BlockSpec((None, d, tf), lambda i, j: (layer, 0, j)),
            pl.BlockSpec((None, d, tf), lambda i, j: (layer, 0, j)),
            pl.BlockSpec((None, tf, d), lambda i, j: (layer, j, 0)),
            _resident((1, d), lambda i, j: (0, 0)),
        ],
        out_specs=pl.BlockSpec((tm, d), lambda i, j: (i, 0)),
        out_shape=jax.ShapeDtypeStruct((m, d), F32),
        scratch_shapes=[pltpu.VMEM((tm, d), BF16)],
        compiler_params=_cparams(("parallel", "arbitrary")),
        name="ffn",
    )(h, g, w_gate, w_up, w_down, g_final)


def _pad_rows(x, rows):
    return jnp.pad(x, ((0, 0), (0, rows - x.shape[1]), (0, 0)))


def _neg_tri_matrix():
    idx = np.arange(TRI)
    return jnp.asarray(-(idx[:, None] >= idx[None, :]).astype(np.float32), dtype=BF16)


def _layer0_groups(pool_w, fox_w):
    outs = [("f32",), ("qscaled",), ("f32", "bf16"), ("f32", "bf16")]
    groups, out_defs, col = [], [], 0
    for kinds in outs:
        width = pool_w if col == 0 else fox_w
        o = []
        for kind in kinds:
            o.append((len(out_defs), 0, kind))
            out_defs.append((width, F32 if kind == "f32" else BF16))
        groups.append((col, width, tuple(o)))
        col += width
    groups.append((None, LANES, ((len(out_defs), 0, "logsig"),)))
    out_defs.append((LANES, F32))
    return tuple(groups), out_defs


def _layer1_groups(d):
    half = d // 2
    groups = []
    for part, outs in enumerate([(0,), (1, 2), (3, 4)]):
        for c in range(2):
            kinds = ("qscaled",) if part == 0 else ("f32", "bf16")
            groups.append((part * d + c * half, half,
                           tuple((oi, c * half, kind) for oi, kind in zip(outs, kinds))))
    out_defs = [(d, BF16), (d, F32), (d, BF16), (d, F32), (d, BF16)]
    return tuple(groups), out_defs


def _run_trunk(x, n_past, pool_c, fk_c, fv_c, flf_c, sk_c, sv_c, wts):
    b, t, d = x.shape
    m = b * t
    h = x.reshape(m, d)
    pool_w = wts["w_pool"].shape[0] * wts["w_pool"].shape[1]
    fox_w = d - pool_w
    fox_h = fox_w // HEAD_DIM
    sb_h = d // HEAD_DIM
    first = fk_c is None
    zero_bias = jnp.zeros((1, LANES), F32)

    groups, out_defs = _layer0_groups(pool_w, fox_w)
    u, q, k, kb, v, vb, lf = norm_proj(h, wts["ln_mix0"], wts["w_in"], wts["w_forget"], wts["b_forget"], groups,
                                        out_defs, tm=256)
    log_f = lf[:, :fox_h]
    if first:
        prefix = jnp.zeros((b, POOL_HALO, pool_w), F32)
    else:
        prefix = jnp.pad(pool_c, ((0, 0), (POOL_HALO - POOL_STATE, 0), (0, 0)))
    y_pool = pool_mixer(u.reshape(b, t, pool_w), prefix, wts["w_pool"], wts["pool_scale"], n_past)
    if t >= POOL_STATE:
        pool_state = u.reshape(b, t, pool_w)[:, t - POOL_STATE:]
    else:
        pool_state = jnp.concatenate([prefix[:, 1:], u.reshape(b, t, pool_w)], axis=1)[:, -POOL_STATE:]

    if first:
        bs = min(512, t)
        tq = min(4 * bs, t)
        cum2 = cumsum_rows(lf)[:, :fox_h].T * LOG2E
        ck = cum2.reshape(fox_h, t // bs, 1, bs)
        o = fox_prompt(q, kb, vb, ck, ck[:, :, 0, bs - 1], tq, bs)
    else:
        n_new = LANES
        lf_new = log_f.reshape(b, t, fox_h).transpose(1, 0, 2).reshape(t, b * fox_h)
        lf_all = jnp.concatenate([flf_c.transpose(1, 0, 2).reshape(n_past, b * fox_h), lf_new], axis=0)
        rows = n_past + t
        rows_pad = -(-rows // 512) * 512
        lf_all = jnp.pad(lf_all, ((0, rows_pad - rows), (0, LANES - b * fox_h)))
        cum2 = cumsum_rows(lf_all)[:rows, :b * fox_h].reshape(rows, b, fox_h).transpose(1, 2, 0) * LOG2E
        ck_cache = cum2[:, :, None, :n_past]
        cum_new = cum2[:, :, n_past:]
        ck_new = jnp.pad(cum_new, ((0, 0), (0, 0), (0, n_new - t)))[:, :, None, :]
        cq = jnp.broadcast_to(cum_new[:, :, :, None], (b, fox_h, t, LANES))
        o = fox_decode(q.reshape(b, t, fox_w), _pad_rows(kb.reshape(b, t, fox_w), n_new),
                       _pad_rows(vb.reshape(b, t, fox_w), n_new), fk_c, fv_c, cq, ck_cache, ck_new, fox_h)
        o = o.reshape(m, fox_w)
    h = out_proj(y_pool.reshape(m, pool_w), o, 0, wts["w_out_ab"], h)
    h = ffn(h, wts["ln_ffn0"], wts["w_gate"], wts["w_up"], wts["w_down"], 0, wts["ln_final"], False)

    groups1, out_defs1 = _layer1_groups(d)
    q1, k1, k1b, v1, v1b = norm_proj(h, wts["ln_mix1"], wts["w_qkv"], wts["w_forget"], zero_bias, groups1, out_defs1,
                                     tm=256)
    ntri = _neg_tri_matrix()
    if first:
        o1 = sb_prompt(q1, k1b, v1b, ntri)
    else:
        n_new = LANES
        o1 = sb_decode(q1.reshape(b, t, d), _pad_rows(k1b.reshape(b, t, d), n_new),
                       _pad_rows(v1b.reshape(b, t, d), n_new), sk_c, sv_c, ntri, sb_h).reshape(m, d)
    h = out_proj(o1, o1, 1, wts["w_out_sb"], h)
    y = ffn(h, wts["ln_ffn1"], wts["w_gate"], wts["w_up"], wts["w_down"], 1, wts["ln_final"], True)

    return (y.reshape(b, t, d), pool_state[None], k.reshape(1, b, t, fox_h, HEAD_DIM),
            v.reshape(1, b, t, fox_h, HEAD_DIM), log_f.reshape(1, b, t, fox_h),
            k1.reshape(1, b, t, sb_h, HEAD_DIM), v1.reshape(1, b, t, sb_h, HEAD_DIM))


def kernel(x_prompt, x_sample, cache_pool, cache_fox_k, cache_fox_v, cache_fox_logf, cache_sb_k, cache_sb_v, ln_mix, w_in_ab, b_forget, w_pool, pool_scale, w_out_ab, w_qkv_sb, w_out_sb, ln_ffn, w_gate, w_up, w_down, ln_final):
    fox_h = b_forget.shape[-1]
    n_main = w_in_ab.shape[-1] - fox_h
    wts = {
        "ln_mix0": ln_mix[0][None], "ln_mix1": ln_mix[1][None],
        "ln_ffn0": ln_ffn[0][None], "ln_ffn1": ln_ffn[1][None], "ln_final": ln_final[None],
        "w_in": w_in_ab[0, :, :n_main].astype(BF16),
        "w_forget": jnp.pad(w_in_ab[0, :, n_main:], ((0, 0), (0, LANES - fox_h))).astype(BF16),
        "b_forget": jnp.pad(b_forget[0], (0, LANES - fox_h))[None],
        "w_pool": w_pool[0].astype(BF16), "pool_scale": pool_scale[0][None],
        "w_out_ab": w_out_ab[0].astype(BF16), "w_qkv": w_qkv_sb[0].astype(BF16),
        "w_out_sb": w_out_sb[0].astype(BF16),
        "w_gate": w_gate.astype(BF16), "w_up": w_up.astype(BF16), "w_down": w_down.astype(BF16),
    }
    out_p = _run_trunk(x_prompt, 0, None, None, None, None, None, None, wts)
    n_past = cache_fox_k.shape[2]
    bs = x_sample.shape[0]

    def rows_of(cache):
        return cache.reshape(bs, -1, HEAD_DIM)

    out_s = _run_trunk(x_sample, n_past, cache_pool[0], rows_of(cache_fox_k), rows_of(cache_fox_v),
                       cache_fox_logf[0], rows_of(cache_sb_k), rows_of(cache_sb_v), wts)
    return (out_p[0], out_s[0]) + out_p[1:] + out_s[1:]
```

```python
import functools

import numpy as np
import jax
import jax.numpy as jnp
from jax import lax
from jax.experimental import pallas as pl
from jax.experimental.pallas import tpu as pltpu

HEAD_DIM = 128
EPS = 1e-6
POOL_WINDOWS = (2, 4, 8, 16)
POOL_STATE = max(POOL_WINDOWS) - 1
POOL_HALO = 16
LANES = 128
TRI = 256
VMEM_LIMIT = 56 * 1024 * 1024
LOG2E = 1.4426950408889634
QK_SCALE2 = HEAD_DIM ** -0.5 * LOG2E

F32 = jnp.float32
BF16 = jnp.bfloat16


def _cparams(sem, vmem=VMEM_LIMIT):
    return pltpu.CompilerParams(dimension_semantics=sem, vmem_limit_bytes=vmem)


def _resident(shape, index_map):
    return pl.BlockSpec(shape, index_map, pipeline_mode=pl.Buffered(1))


def _dot(a, b):
    return jnp.dot(a, b, preferred_element_type=F32)


def _dot_nt(a, b):
    return lax.dot_general(a, b, (((1,), (1,)), ((), ())), preferred_element_type=F32)


def _rms(x, g):
    ms = jnp.mean(x * x, axis=-1, keepdims=True)
    return x * lax.rsqrt(ms + EPS) * g


def _softplus(z):
    return jnp.maximum(z, 0.0) + jnp.log1p(jnp.exp(-jnp.abs(z)))


def _softplus2(z2):
    return jnp.maximum(z2, 0.0) + jnp.log(1.0 + jnp.exp2(-jnp.abs(z2))) * LOG2E


def _tile_lanes(x, n):
    reps = n // LANES
    return x if reps == 1 else jnp.concatenate([x] * reps, axis=1)


def _head_cols(h):
    return slice(h * HEAD_DIM, (h + 1) * HEAD_DIM)


def _head_rows(ref, h, heads, n):
    return ref[pl.ds(h, n, stride=heads), :]


def _norm_proj_kernel(x_ref, g_ref, w_ref, w2_ref, b_ref, *out_refs, groups):
    xn = _rms(x_ref[...], g_ref[...]).astype(BF16)
    for w_col, width, outs in groups:
        acc = _dot(xn, w2_ref[...] if w_col is None else w_ref[:, w_col:w_col + width])
        for out_idx, out_col, kind in outs:
            if kind == "logsig":
                val = -_softplus(-(acc + b_ref[...]))
            elif kind == "qscaled":
                val = acc * QK_SCALE2
            else:
                val = acc
            o = out_refs[out_idx]
            o[:, out_col:out_col + width] = val.astype(o.dtype)


def norm_proj(x, g, w, w2, bias, groups, out_defs, tm):
    m, d = x.shape
    n = w.shape[1]
    tm = min(tm, m)
    out_shape = [jax.ShapeDtypeStruct((m, c), dt) for c, dt in out_defs]
    out_specs = [pl.BlockSpec((tm, c), lambda i: (i, 0)) for c, _ in out_defs]
    return pl.pallas_call(
        functools.partial(_norm_proj_kernel, groups=groups),
        grid=(m // tm,),
        in_specs=[
            pl.BlockSpec((tm, d), lambda i: (i, 0)),
            _resident((1, d), lambda i: (0, 0)),
            _resident((d, n), lambda i: (0, 0)),
            _resident((d, LANES), lambda i: (0, 0)),
            _resident((1, LANES), lambda i: (0, 0)),
        ],
        out_specs=out_specs,
        out_shape=out_shape,
        compiler_params=_cparams(("parallel",)),
        name="norm_proj",
    )(x, g, w, w2, bias)


def _cumsum_kernel(x_ref, o_ref, carry_ref, *, tb):
    @pl.when(pl.program_id(0) == 0)
    def _():
        carry_ref[...] = jnp.zeros_like(carry_ref)

    x = x_ref[...]
    row = lax.broadcasted_iota(jnp.int32, (tb, tb), 0)
    col = lax.broadcasted_iota(jnp.int32, (tb, tb), 1)
    low = jnp.where(col <= row, 1.0, 0.0).astype(BF16)
    hi = x.astype(BF16)
    r1 = x - hi.astype(F32)
    mid = r1.astype(BF16)
    lo = (r1 - mid.astype(F32)).astype(BF16)
    out = _dot(low, hi) + _dot(low, mid) + _dot(low, lo) + carry_ref[0:1, :]
    o_ref[...] = out
    carry_ref[...] = jnp.broadcast_to(out[tb - 1:tb, :], carry_ref.shape)


def cumsum_rows(x, tb=512):
    t, c = x.shape
    tb = min(tb, t)
    return pl.pallas_call(
        functools.partial(_cumsum_kernel, tb=tb),
        grid=(t // tb,),
        in_specs=[pl.BlockSpec((tb, c), lambda i: (i, 0))],
        out_specs=pl.BlockSpec((tb, c), lambda i: (i, 0)),
        out_shape=jax.ShapeDtypeStruct((t, c), F32),
        scratch_shapes=[pltpu.VMEM((8, c), F32)],
        compiler_params=_cparams(("arbitrary",)),
        name="cumsum_rows",
    )(x)


def _pool_kernel(u_ref, pre_ref, w_ref, sc_ref, o_ref, buf_ref, *, tm, pos0, group):
    mi = pl.program_id(1)

    @pl.when(mi == 0)
    def _():
        buf_ref[0:POOL_HALO, :] = pre_ref[...]

    @pl.when(mi > 0)
    def _():
        buf_ref[0:POOL_HALO, :] = buf_ref[tm:tm + POOL_HALO, :]

    buf_ref[POOL_HALO:POOL_HALO + tm, :] = u_ref[...]
    pos = pos0 + mi * tm + lax.broadcasted_iota(jnp.int32, (tm, 1), 0)
    for gi, win in enumerate(POOL_WINDOWS):
        cs = slice(gi * group, (gi + 1) * group)
        tot = buf_ref[POOL_HALO:POOL_HALO + tm, cs]
        for back in range(1, win):
            tot = tot + buf_ref[POOL_HALO - back:POOL_HALO - back + tm, cs]
        count = jnp.minimum(pos + 1, win).astype(F32)
        diff = tot / count - u_ref[:, cs]
        y = _dot(diff.astype(BF16), w_ref[gi]) * sc_ref[:, cs]
        o_ref[:, cs] = y.astype(o_ref.dtype)


def pool_mixer(u, prefix, w_pool, scale, pos0, tm=512):
    b, t, c = u.shape
    tm = min(tm, t)
    group = c // len(POOL_WINDOWS)
    return pl.pallas_call(
        functools.partial(_pool_kernel, tm=tm, pos0=pos0, group=group),
        grid=(b, t // tm),
        in_specs=[
            pl.BlockSpec((None, tm, c), lambda bi, mi: (bi, mi, 0)),
            pl.BlockSpec((None, POOL_HALO, c), lambda bi, mi: (bi, 0, 0)),
            _resident(w_pool.shape, lambda bi, mi: (0, 0, 0)),
            _resident((1, c), lambda bi, mi: (0, 0)),
        ],
        out_specs=pl.BlockSpec((None, tm, c), lambda bi, mi: (bi, mi, 0)),
        out_shape=jax.ShapeDtypeStruct((b, t, c), BF16),
        scratch_shapes=[pltpu.VMEM((POOL_HALO + tm, c), F32)],
        compiler_params=_cparams(("arbitrary", "arbitrary")),
        name="pool_mixer",
    )(u, prefix, w_pool, scale)


def _fox_update(s2, v, m_prev, l_prev, acc_prev):
    m_new = jnp.maximum(m_prev, jnp.max(s2, axis=1, keepdims=True))
    alpha = jnp.exp2(m_prev - m_new)
    p = jnp.exp2(s2 - _tile_lanes(m_new, s2.shape[1]))
    l_new = alpha * l_prev + jnp.sum(p, axis=1, keepdims=True)
    acc_new = alpha * acc_prev + _dot(p.astype(BF16), v)
    return m_new, l_new, acc_new


FOX_DEAD = -160.0


def _fox_prompt_kernel(ck_end_ref, q_ref, k_ref, v_ref, ck_ref, o_ref, m_ref, l_ref, acc_ref, cq_ref, kmax_ref,
                       *, tq, bs):
    hi = pl.program_id(0)
    qi = pl.program_id(1)
    nq = pl.num_programs(1)
    nsub = tq // bs

    @pl.when(qi == 0)
    def _():
        def norm_body(j, best):
            kb = k_ref[pl.ds(pl.multiple_of(j * tq, tq), tq), :].astype(F32)
            return jnp.maximum(best, jnp.sum(kb * kb, axis=1, keepdims=True))

        best = lax.fori_loop(0, nq, norm_body, jnp.zeros((tq, 1), F32))
        kmax_ref[...] = jnp.broadcast_to(jnp.sqrt(jnp.max(best, axis=0, keepdims=True)), kmax_ref.shape)

    row = lax.broadcasted_iota(jnp.int32, (bs, bs), 0)
    col = lax.broadcasted_iota(jnp.int32, (bs, bs), 1)

    def sub_rows(s):
        return slice(s * bs, (s + 1) * bs)

    for s in range(nsub):
        own = jnp.where(row == col, jnp.broadcast_to(ck_ref[qi * nsub + s], (bs, bs)), 0.0)
        cq_ref[sub_rows(s), :] = jnp.broadcast_to(jnp.sum(own, axis=1, keepdims=True), (bs, LANES))

    def update(s, j, nblk, causal, fresh=False):
        width = nblk * bs
        rows = pl.ds(pl.multiple_of(j * bs, bs), width)
        cq = cq_ref[sub_rows(s), :]
        ck = ck_ref[j] if nblk == 1 else jnp.concatenate([ck_ref[j + i] for i in range(nblk)], axis=1)
        s2 = _dot_nt(q_ref[sub_rows(s), :], k_ref[rows, :]) + _tile_lanes(cq, width) - ck
        if causal:
            tail = jnp.where(col <= row, s2[:, width - bs:], -jnp.inf)
            s2 = tail if nblk == 1 else jnp.concatenate([s2[:, :width - bs], tail], axis=1)
        if fresh:
            state = (jnp.full((bs, LANES), -jnp.inf, F32), jnp.zeros((bs, LANES), F32), jnp.zeros((bs, HEAD_DIM), F32))
        else:
            state = (m_ref[s], l_ref[s], acc_ref[s])
        m_ref[s], l_ref[s], acc_ref[s] = _fox_update(s2, v_ref[rows, :], *state)

    def first_units(at_origin):
        for s in range(nsub):
            if s == 0 and at_origin:
                update(s, 0, 1, True, fresh=True)
            else:
                update(s, qi * nsub + s - 1, 2, True, fresh=True)

    @pl.when(qi == 0)
    def _():
        first_units(True)

    @pl.when(qi > 0)
    def _():
        first_units(False)

    def rest(s):
        g = qi * nsub + s
        qf = q_ref[sub_rows(s), :].astype(F32)
        reach = (jnp.sqrt(jnp.sum(qf * qf, axis=1, keepdims=True)) * kmax_ref[0:1, 0:1]
                 + cq_ref[sub_rows(s), 0:1])

        def margin():
            return jnp.max(reach - m_ref[s][:, 0:1])

        def live(gap, j_last):
            return gap - ck_end_ref[hi, jnp.maximum(j_last, 0)] > FOX_DEAD

        n_rest = jnp.maximum(g - 1, 0)
        gap0 = margin()
        pair = n_rest >= 2

        @pl.when(jnp.logical_and(pair, live(gap0, n_rest - 1)))
        def _():
            update(s, n_rest - 2, 2, False)

        n_left = jnp.where(pair, n_rest - 2, n_rest)

        def cond(state):
            i, alive = state
            return jnp.logical_and(i < n_left // 4, alive)

        def body(state):
            i, _ = state
            update(s, n_left - 4 - 4 * i, 4, False)
            return i + 1, live(margin(), n_left - 5 - 4 * i)

        _, alive = lax.while_loop(cond, body, (0, live(gap0, n_left - 1)))
        rem = n_left % 4

        @pl.when(jnp.logical_and(rem >= 2, alive))
        def _():
            update(s, rem % 2, 2, False)

        @pl.when(jnp.logical_and(rem % 2 == 1, alive))
        def _():
            update(s, 0, 1, False)

        o_ref[sub_rows(s), :] = (acc_ref[s] / l_ref[s]).astype(o_ref.dtype)

    for s in range(nsub):
        rest(s)


def fox_prompt(q, k, v, ck, ck_end, tq, bs):
    t, hd = q.shape
    h = hd // HEAD_DIM
    nq = t // tq
    return pl.pallas_call(
        functools.partial(_fox_prompt_kernel, tq=tq, bs=bs),
        grid=(h, nq),
        in_specs=[
            pl.BlockSpec(memory_space=pltpu.SMEM),
            pl.BlockSpec((tq, HEAD_DIM), lambda hi, qi: (qi, hi)),
            pl.BlockSpec((t, HEAD_DIM), lambda hi, qi: (0, hi)),
            pl.BlockSpec((t, HEAD_DIM), lambda hi, qi: (0, hi)),
            pl.BlockSpec((None, t // bs, 1, bs), lambda hi, qi: (hi, 0, 0, 0)),
        ],
        out_specs=pl.BlockSpec((tq, HEAD_DIM), lambda hi, qi: (qi, hi)),
        out_shape=jax.ShapeDtypeStruct((t, hd), BF16),
        scratch_shapes=[pltpu.VMEM((tq // bs, bs, LANES), F32), pltpu.VMEM((tq // bs, bs, LANES), F32),
                        pltpu.VMEM((tq // bs, bs, HEAD_DIM), F32), pltpu.VMEM((tq, LANES), F32),
                        pltpu.VMEM((8, LANES), F32)],
        compiler_params=_cparams(("arbitrary", "arbitrary")),
        name="fox_prompt",
    )(ck_end, q, k, v, ck)


def _fox_decode_kernel(q_ref, kn_ref, vn_ref, kc_ref, vc_ref, cq_ref, ckc_ref, ckn_ref, o_ref,
                       m_ref, l_ref, acc_ref, *, heads, tk):
    kb = pl.program_id(1)
    tq = q_ref.shape[0]
    n_new = kn_ref.shape[0]

    @pl.when(kb == 0)
    def _():
        row = lax.broadcasted_iota(jnp.int32, (tq, n_new), 0)
        col = lax.broadcasted_iota(jnp.int32, (tq, n_new), 1)
        for h in range(heads):
            hc = _head_cols(h)
            s2 = _dot_nt(q_ref[:, hc], kn_ref[:, hc]) + _tile_lanes(cq_ref[h], n_new) - ckn_ref[h]
            s2 = jnp.where(col <= row, s2, -jnp.inf)
            m = jnp.max(s2, axis=1, keepdims=True)
            p = jnp.exp2(s2 - m)
            m_ref[h] = jnp.broadcast_to(m, (tq, LANES))
            l_ref[h] = jnp.broadcast_to(jnp.sum(p, axis=1, keepdims=True), (tq, LANES))
            acc_ref[h] = _dot(p.astype(BF16), vn_ref[:, hc])

    for h in range(heads):
        k_h = _head_rows(kc_ref, h, heads, tk).astype(BF16)
        v_h = _head_rows(vc_ref, h, heads, tk).astype(BF16)
        s2 = _dot_nt(q_ref[:, _head_cols(h)], k_h) + _tile_lanes(cq_ref[h], tk) - ckc_ref[h]
        m_ref[h], l_ref[h], acc_ref[h] = _fox_update(s2, v_h, m_ref[h], l_ref[h], acc_ref[h])

    @pl.when(kb == pl.num_programs(1) - 1)
    def _():
        for h in range(heads):
            o_ref[:, _head_cols(h)] = (acc_ref[h] / l_ref[h]).astype(o_ref.dtype)


def fox_decode(q, k_new, v_new, k_cache, v_cache, cq, ck_cache, ck_new, heads, tk=2048):
    b, tq, hd = q.shape
    n_new = k_new.shape[1]
    p_len = k_cache.shape[1] // heads
    tk = min(tk, p_len)
    return pl.pallas_call(
        functools.partial(_fox_decode_kernel, heads=heads, tk=tk),
        grid=(b, p_len // tk),
        in_specs=[
            pl.BlockSpec((None, tq, hd), lambda bi, kb: (bi, 0, 0)),
            pl.BlockSpec((None, n_new, hd), lambda bi, kb: (bi, 0, 0)),
            pl.BlockSpec((None, n_new, hd), lambda bi, kb: (bi, 0, 0)),
            pl.BlockSpec((None, tk * heads, HEAD_DIM), lambda bi, kb: (bi, kb, 0)),
            pl.BlockSpec((None, tk * heads, HEAD_DIM), lambda bi, kb: (bi, kb, 0)),
            pl.BlockSpec((None, heads, tq, LANES), lambda bi, kb: (bi, 0, 0, 0)),
            pl.BlockSpec((None, heads, 1, tk), lambda bi, kb: (bi, 0, 0, kb)),
            pl.BlockSpec((None, heads, 1, n_new), lambda bi, kb: (bi, 0, 0, 0)),
        ],
        out_specs=pl.BlockSpec((None, tq, hd), lambda bi, kb: (bi, 0, 0)),
        out_shape=jax.ShapeDtypeStruct((b, tq, hd), BF16),
        scratch_shapes=[pltpu.VMEM((heads, tq, LANES), F32), pltpu.VMEM((heads, tq, LANES), F32),
                        pltpu.VMEM((heads, tq, HEAD_DIM), F32)],
        compiler_params=_cparams(("parallel", "arbitrary")),
        name="fox_decode",
    )(q, k_new, v_new, k_cache, v_cache, cq, ck_cache, ck_new)


SB_DEAD = -160.0


def _suffix_sums(sp, ntri, carry):
    width = ntri.shape[0]
    n = sp.shape[1]
    parts = []
    for c0 in range(n - width, -1, -width):
        piece = sp[:, c0:c0 + width]
        hi = piece.astype(BF16)
        lo = (piece - hi.astype(F32)).astype(BF16)
        c = _dot(hi, ntri) + _dot(lo, ntri) + carry
        carry = c[:, 0:1]
        parts.append(c)
    parts.reverse()
    return (parts[0] if len(parts) == 1 else jnp.concatenate(parts, axis=1)), carry


def _mask_tail(x, mask):
    if mask is None:
        return x
    head = x.shape[1] - mask.shape[1]
    tail = jnp.where(mask, x[:, head:], 0.0)
    return tail if head == 0 else jnp.concatenate([x[:, :head], tail], axis=1)


def _sb_weights(z2, ntri, carry, mask):
    sp = _mask_tail(_softplus2(z2), mask)
    c, new_carry = _suffix_sums(sp, ntri, carry)
    a = _mask_tail(jnp.exp2(z2 + c), mask)
    return a.astype(BF16), new_carry


def _sb_prompt_kernel(q_ref, k_ref, v_ref, ntri_ref, o_ref, r_ref, acc_ref, *, tq, bs):
    qi = pl.program_id(1)
    nsub = tq // bs
    ntri = ntri_ref[...]
    row = lax.broadcasted_iota(jnp.int32, (bs, bs), 0)
    col = lax.broadcasted_iota(jnp.int32, (bs, bs), 1)

    def sub_rows(s):
        return slice(s * bs, (s + 1) * bs)

    def block(s, j, nblk, carry, mask):
        rows = pl.ds(pl.multiple_of(j * bs, bs), nblk * bs)
        a, carry = _sb_weights(_dot_nt(q_ref[sub_rows(s), :], k_ref[rows, :]), ntri, carry, mask)
        return _dot(a, v_ref[rows, :]), carry

    def first_units(at_origin):
        for s in range(nsub):
            if s == 0 and at_origin:
                out, carry = block(s, 0, 1, jnp.zeros((bs, 1), F32), col < row)
            else:
                out, carry = block(s, qi * nsub + s - 1, 2, jnp.zeros((bs, 1), F32), col < row)
            acc_ref[s] = out
            r_ref[s] = jnp.broadcast_to(carry, (bs, LANES))

    @pl.when(qi == 0)
    def _():
        first_units(True)

    @pl.when(qi > 0)
    def _():
        first_units(False)

    def live(s):
        return jnp.max(r_ref[s]) > SB_DEAD

    alive_after_first = [live(s) for s in range(nsub)]

    def rest(s):
        g = qi * nsub + s

        def step(j, nblk):
            out, carry = block(s, j, nblk, r_ref[s][:, 0:1], None)
            acc_ref[s] += out
            r_ref[s] = jnp.broadcast_to(carry, (bs, LANES))

        n_rest = jnp.maximum(g - 1, 0)

        def cond(state):
            i, alive = state
            return jnp.logical_and(i < n_rest // 4, alive)

        def body(state):
            i, _ = state
            step(n_rest - 4 - 4 * i, 4)
            return i + 1, live(s)

        _, alive = lax.while_loop(cond, body, (0, alive_after_first[s]))

        @pl.when(jnp.logical_and(n_rest % 4 >= 2, alive))
        def _():
            step(n_rest % 2, 2)

        @pl.when(jnp.logical_and(n_rest % 2 == 1, alive))
        def _():
            step(0, 1)

        o_ref[sub_rows(s), :] = acc_ref[s].astype(o_ref.dtype)

    for s in range(nsub):
        rest(s)


def sb_prompt(q, k, v, ntri, tq=2048, bs=256):
    t, hd = q.shape
    h = hd // HEAD_DIM
    tq = min(tq, t)
    bs = min(bs, tq)
    return pl.pallas_call(
        functools.partial(_sb_prompt_kernel, tq=tq, bs=bs),
        grid=(h, t // tq),
        in_specs=[
            pl.BlockSpec((tq, HEAD_DIM), lambda hi, qi: (qi, hi)),
            pl.BlockSpec((t, HEAD_DIM), lambda hi, qi: (0, hi)),
            pl.BlockSpec((t, HEAD_DIM), lambda hi, qi: (0, hi)),
            _resident(ntri.shape, lambda hi, qi: (0, 0)),
        ],
        out_specs=pl.BlockSpec((tq, HEAD_DIM), lambda hi, qi: (qi, hi)),
        out_shape=jax.ShapeDtypeStruct((t, hd), BF16),
        scratch_shapes=[pltpu.VMEM((tq // bs, bs, LANES), F32), pltpu.VMEM((tq // bs, bs, HEAD_DIM), F32)],
        compiler_params=_cparams(("parallel", "arbitrary")),
        name="sb_prompt",
    )(q, k, v, ntri)


def _sb_decode_kernel(q_ref, kn_ref, vn_ref, kc_hbm, vc_hbm, ntri_ref, o_ref, kbuf, vbuf, sem, r_ref, acc_ref,
                      *, heads, tk):
    b = pl.program_id(0)
    tq = q_ref.shape[0]
    n_new = kn_ref.shape[0]
    blk = tk * heads
    nkb = kc_hbm.shape[1] // blk

    def fetch(j, slot):
        rows = pl.ds(pl.multiple_of(j * blk, blk), blk)
        return (pltpu.make_async_copy(kc_hbm.at[b, rows, :], kbuf.at[slot], sem.at[0, slot]),
                pltpu.make_async_copy(vc_hbm.at[b, rows, :], vbuf.at[slot], sem.at[1, slot]))

    for copy in fetch(nkb - 1, 0):
        copy.start()

    def head_rows(h):
        return slice(h * tq, (h + 1) * tq)

    def stacked_scores(keys_of):
        return jnp.concatenate([_dot_nt(q_ref[:, _head_cols(h)], keys_of(h)) for h in range(heads)], axis=0)

    def live():
        return jnp.max(r_ref[...]) > SB_DEAD

    z2 = stacked_scores(lambda h: kn_ref[:, _head_cols(h)])
    row = lax.rem(lax.broadcasted_iota(jnp.int32, (heads * tq, n_new), 0), tq)
    col = lax.broadcasted_iota(jnp.int32, (heads * tq, n_new), 1)
    a, carry = _sb_weights(z2, ntri_ref[0:n_new, 0:n_new], jnp.zeros((heads * tq, 1), F32), col < row)
    for h in range(heads):
        acc_ref[head_rows(h), :] = _dot(a[head_rows(h), :], vn_ref[:, _head_cols(h)])
    r_ref[...] = jnp.broadcast_to(carry, r_ref.shape)

    def body(state):
        i, _ = state
        j = nkb - 1 - i
        slot = lax.rem(i, 2)
        for copy in fetch(j, slot):
            copy.wait()

        @pl.when(j > 0)
        def _():
            for copy in fetch(j - 1, 1 - slot):
                copy.start()

        z2 = stacked_scores(lambda h: _head_rows(kbuf.at[slot], h, heads, tk).astype(BF16))
        a, carry = _sb_weights(z2, ntri_ref[...], r_ref[:, 0:1], None)
        for h in range(heads):
            acc_ref[head_rows(h), :] += _dot(a[head_rows(h), :], _head_rows(vbuf.at[slot], h, heads, tk).astype(BF16))
        r_ref[...] = jnp.broadcast_to(carry, r_ref.shape)
        return i + 1, live()

    n_done, _ = lax.while_loop(lambda st: jnp.logical_and(st[0] < nkb, st[1]), body, (0, live()))

    @pl.when(n_done < nkb)
    def _():
        for copy in fetch(nkb - 1 - n_done, lax.rem(n_done, 2)):
            copy.wait()

    for h in range(heads):
        o_ref[:, _head_cols(h)] = acc_ref[head_rows(h), :].astype(o_ref.dtype)


def sb_decode(q, k_new, v_new, k_cache, v_cache, ntri, heads, tk=256):
    b, tq, hd = q.shape
    n_new = k_new.shape[1]
    p_len = k_cache.shape[1] // heads
    tk = min(tk, p_len)
    return pl.pallas_call(
        functools.partial(_sb_decode_kernel, heads=heads, tk=tk),
        grid=(b,),
        in_specs=[
            pl.BlockSpec((None, tq, hd), lambda bi: (bi, 0, 0)),
            pl.BlockSpec((None, n_new, hd), lambda bi: (bi, 0, 0)),
            pl.BlockSpec((None, n_new, hd), lambda bi: (bi, 0, 0)),
            pl.BlockSpec(memory_space=pl.ANY),
            pl.BlockSpec(memory_space=pl.ANY),
            _resident(ntri.shape, lambda bi: (0, 0)),
        ],
        out_specs=pl.BlockSpec((None, tq, hd), lambda bi: (bi, 0, 0)),
        out_shape=jax.ShapeDtypeStruct((b, tq, hd), BF16),
        scratch_shapes=[pltpu.VMEM((2, tk * heads, HEAD_DIM), F32), pltpu.VMEM((2, tk * heads, HEAD_DIM), F32),
                        pltpu.SemaphoreType.DMA((2, 2)),
                        pltpu.VMEM((heads * tq, LANES), F32), pltpu.VMEM((heads * tq, HEAD_DIM), F32)],
        compiler_params=_cparams(("arbitrary",)),
        name="sb_decode",
    )(q, k_new, v_new, k_cache, v_cache, ntri)


def _out_proj_kernel(xa_ref, xb_ref, w_ref, h_ref, o_ref):
    ka = xa_ref.shape[1]
    o_ref[...] = h_ref[...] + _dot(xa_ref[...], w_ref[:ka, :]) + _dot(xb_ref[...], w_ref[ka:, :])


def out_proj(xa, xb, col_b, w, h, tm=512):
    m, d = h.shape
    ka = w.shape[0] // 2
    tm = min(tm, m)
    return pl.pallas_call(
        _out_proj_kernel,
        grid=(m // tm,),
        in_specs=[
            pl.BlockSpec((tm, ka), lambda i: (i, 0)),
            pl.BlockSpec((tm, ka), lambda i: (i, col_b)),
            _resident(w.shape, lambda i: (0, 0)),
            pl.BlockSpec((tm, d), lambda i: (i, 0)),
        ],
        out_specs=pl.BlockSpec((tm, d), lambda i: (i, 0)),
        out_shape=jax.ShapeDtypeStruct((m, d), F32),
        compiler_params=_cparams(("parallel",)),
        name="out_proj",
    )(xa, xb, w, h)


def _ffn_kernel(h_ref, g_ref, wg_ref, wu_ref, wd_ref, gf_ref, o_ref, xn_ref, *, final_norm):
    fi = pl.program_id(1)

    @pl.when(fi == 0)
    def _():
        h = h_ref[...]
        xn_ref[...] = _rms(h, g_ref[...]).astype(BF16)
        o_ref[...] = h

    xn = xn_ref[...]
    gate = _dot(xn, wg_ref[...])
    up = _dot(xn, wu_ref[...])
    mid = (gate / (1.0 + jnp.exp(-gate)) * up).astype(BF16)
    width = wd_ref.shape[0]
    for c0 in range(0, o_ref.shape[1], width):
        o_ref[:, c0:c0 + width] += _dot(mid, wd_ref[:, c0:c0 + width])

    if final_norm:
        @pl.when(fi == pl.num_programs(1) - 1)
        def _():
            o_ref[...] = _rms(o_ref[...], gf_ref[...])


def ffn(h, g, w_gate, w_up, w_down, layer, g_final, final_norm, tm=1024, tf=512):
    m, d = h.shape
    f = w_gate.shape[2]
    tm = min(tm, m)
    return pl.pallas_call(
        functools.partial(_ffn_kernel, final_norm=final_norm),
        grid=(m // tm, f // tf),
        in_specs=[
            pl.BlockSpec((tm, d), lambda i, j: (i, 0), pipeline_mode=pl.Buffered(1)),
            _resident((1, d), lambda i, j: (0, 0)),
            pl.BlockSpec((None, d, tf), lambda i, j: (layer, 0, j)),
            pl.BlockSpec((None, d, tf), lambda i, j: (layer, 0, j)),
            pl.BlockSpec((None, tf, d), lambda i, j: (layer, j, 0)),
            _resident((1, d), lambda i, j: (0, 0)),
        ],
        out_specs=pl.BlockSpec((tm, d), lambda i, j: (i, 0)),
        out_shape=jax.ShapeDtypeStruct((m, d), F32),
        scratch_shapes=[pltpu.VMEM((tm, d), BF16)],
        compiler_params=_cparams(("parallel", "arbitrary")),
        name="ffn",
    )(h, g, w_gate, w_up, w_down, g_final)


def _pad_rows(x, rows):
    return jnp.pad(x, ((0, 0), (0, rows - x.shape[1]), (0, 0)))


def _neg_tri_matrix():
    idx = np.arange(TRI)
    return jnp.asarray(-(idx[:, None] >= idx[None, :]).astype(np.float32), dtype=BF16)


def _layer0_groups(pool_w, fox_w):
    outs = [("f32",), ("qscaled",), ("f32", "bf16"), ("f32", "bf16")]
    groups, out_defs, col = [], [], 0
    for kinds in outs:
        width = pool_w if col == 0 else fox_w
        o = []
        for kind in kinds:
            o.append((len(out_defs), 0, kind))
            out_defs.append((width, F32 if kind == "f32" else BF16))
        groups.append((col, width, tuple(o)))
        col += width
    groups.append((None, LANES, ((len(out_defs), 0, "logsig"),)))
    out_defs.append((LANES, F32))
    return tuple(groups), out_defs


def _layer1_groups(d):
    half = d // 2
    groups = []
    for part, outs in enumerate([(0,), (1, 2), (3, 4)]):
        for c in range(2):
            kinds = ("qscaled",) if part == 0 else ("f32", "bf16")
            groups.append((part * d + c * half, half,
                           tuple((oi, c * half, kind) for oi, kind in zip(outs, kinds))))
    out_defs = [(d, BF16), (d, F32), (d, BF16), (d, F32), (d, BF16)]
    return tuple(groups), out_defs


def _run_trunk(x, n_past, pool_c, fk_c, fv_c, flf_c, sk_c, sv_c, wts):
    b, t, d = x.shape
    m = b * t
    h = x.reshape(m, d)
    pool_w = wts["w_pool"].shape[0] * wts["w_pool"].shape[1]
    fox_w = d - pool_w
    fox_h = fox_w // HEAD_DIM
    sb_h = d // HEAD_DIM
    first = fk_c is None
    zero_bias = jnp.zeros((1, LANES), F32)

    groups, out_defs = _layer0_groups(pool_w, fox_w)
    u, q, k, kb, v, vb, lf = norm_proj(h, wts["ln_mix0"], wts["w_in"], wts["w_forget"], wts["b_forget"], groups,
                                        out_defs, tm=256)
    log_f = lf[:, :fox_h]
    if first:
        prefix = jnp.zeros((b, POOL_HALO, pool_w), F32)
    else:
        prefix = jnp.pad(pool_c, ((0, 0), (POOL_HALO - POOL_STATE, 0), (0, 0)))
    y_pool = pool_mixer(u.reshape(b, t, pool_w), prefix, wts["w_pool"], wts["pool_scale"], n_past)
    if t >= POOL_STATE:
        pool_state = u.reshape(b, t, pool_w)[:, t - POOL_STATE:]
    else:
        pool_state = jnp.concatenate([prefix[:, 1:], u.reshape(b, t, pool_w)], axis=1)[:, -POOL_STATE:]

    if first:
        bs = min(512, t)
        tq = min(4 * bs, t)
        cum2 = cumsum_rows(lf)[:, :fox_h].T * LOG2E
        ck = cum2.reshape(fox_h, t // bs, 1, bs)
        o = fox_prompt(q, kb, vb, ck, ck[:, :, 0, bs - 1], tq, bs)
    else:
        n_new = LANES
        lf_new = log_f.reshape(b, t, fox_h).transpose(1, 0, 2).reshape(t, b * fox_h)
        lf_all = jnp.concatenate([flf_c.transpose(1, 0, 2).reshape(n_past, b * fox_h), lf_new], axis=0)
        rows = n_past + t
        rows_pad = -(-rows // 512) * 512
        lf_all = jnp.pad(lf_all, ((0, rows_pad - rows), (0, LANES - b * fox_h)))
        cum2 = cumsum_rows(lf_all)[:rows, :b * fox_h].reshape(rows, b, fox_h).transpose(1, 2, 0) * LOG2E
        ck_cache = cum2[:, :, None, :n_past]
        cum_new = cum2[:, :, n_past:]
        ck_new = jnp.pad(cum_new, ((0, 0), (0, 0), (0, n_new - t)))[:, :, None, :]
        cq = jnp.broadcast_to(cum_new[:, :, :, None], (b, fox_h, t, LANES))
        o = fox_decode(q.reshape(b, t, fox_w), _pad_rows(kb.reshape(b, t, fox_w), n_new),
                       _pad_rows(vb.reshape(b, t, fox_w), n_new), fk_c, fv_c, cq, ck_cache, ck_new, fox_h)
        o = o.reshape(m, fox_w)
    h = out_proj(y_pool.reshape(m, pool_w), o, 0, wts["w_out_ab"], h)
    h = ffn(h, wts["ln_ffn0"], wts["w_gate"], wts["w_up"], wts["w_down"], 0, wts["ln_final"], False)

    groups1, out_defs1 = _layer1_groups(d)
    q1, k1, k1b, v1, v1b = norm_proj(h, wts["ln_mix1"], wts["w_qkv"], wts["w_forget"], zero_bias, groups1, out_defs1,
                                     tm=256)
    ntri = _neg_tri_matrix()
    if first:
        o1 = sb_prompt(q1, k1b, v1b, ntri)
    else:
        n_new = LANES
        o1 = sb_decode(q1.reshape(b, t, d), _pad_rows(k1b.reshape(b, t, d), n_new),
                       _pad_rows(v1b.reshape(b, t, d), n_new), sk_c, sv_c, ntri, sb_h).reshape(m, d)
    h = out_proj(o1, o1, 1, wts["w_out_sb"], h)
    y = ffn(h, wts["ln_ffn1"], wts["w_gate"], wts["w_up"], wts["w_down"], 1, wts["ln_final"], True)

    return (y.reshape(b, t, d), pool_state[None], k.reshape(1, b, t, fox_h, HEAD_DIM),
            v.reshape(1, b, t, fox_h, HEAD_DIM), log_f.reshape(1, b, t, fox_h),
            k1.reshape(1, b, t, sb_h, HEAD_DIM), v1.reshape(1, b, t, sb_h, HEAD_DIM))


def kernel(x_prompt, x_sample, cache_pool, cache_fox_k, cache_fox_v, cache_fox_logf, cache_sb_k, cache_sb_v, ln_mix, w_in_ab, b_forget, w_pool, pool_scale, w_out_ab, w_qkv_sb, w_out_sb, ln_ffn, w_gate, w_up, w_down, ln_final):
    fox_h = b_forget.shape[-1]
    n_main = w_in_ab.shape[-1] - fox_h
    wts = {
        "ln_mix0": ln_mix[0][None], "ln_mix1": ln_mix[1][None],
        "ln_ffn0": ln_ffn[0][None], "ln_ffn1": ln_ffn[1][None], "ln_final": ln_final[None],
        "w_in": w_in_ab[0, :, :n_main].astype(BF16),
        "w_forget": jnp.pad(w_in_ab[0, :, n_main:], ((0, 0), (0, LANES - fox_h))).astype(BF16),
        "b_forget": jnp.pad(b_forget[0], (0, LANES - fox_h))[None],
        "w_pool": w_pool[0].astype(BF16), "pool_scale": pool_scale[0][None],
        "w_out_ab": w_out_ab[0].astype(BF16), "w_qkv": w_qkv_sb[0].astype(BF16),
        "w_out_sb": w_out_sb[0].astype(BF16),
        "w_gate": w_gate.astype(BF16), "w_up": w_up.astype(BF16), "w_down": w_down.astype(BF16),
    }
    out_p = _run_trunk(x_prompt, 0, None, None, None, None, None, None, wts)
    n_past = cache_fox_k.shape[2]
    bs = x_sample.shape[0]

    def rows_of(cache):
        return cache.reshape(bs, -1, HEAD_DIM)

    out_s = _run_trunk(x_sample, n_past, cache_pool[0], rows_of(cache_fox_k), rows_of(cache_fox_v),
                       cache_fox_logf[0], rows_of(cache_sb_k), rows_of(cache_sb_v), wts)
    return (out_p[0], out_s[0]) + out_p[1:] + out_s[1:]
```
